```python
import math
import jax, jax.numpy as jnp
from jax import lax
import numpy as np

D_MODEL = 1024
BATCH = 16
SEQ = 256
DEPTH = 4
DEC_BATCH = 4
DEC_SEQ = 4096
PAST_LEN = 256

GRID_W = 64
N_MIXERS = 4
RMS_EPS = 1e-6
CONV_W = 3
POOL_WINDOWS = (2, 4, 8, 16)
POOL_GROUP = D_MODEL // len(POOL_WINDOWS)
SGU_CHUNK = 128
SGU_GROUPS = 8
SGU_DIM = D_MODEL // SGU_GROUPS
HGRN_HEADS = 8
HGRN_DK = D_MODEL // HGRN_HEADS
HGRN_DV = D_MODEL // HGRN_HEADS
HGRN_CHUNK = 32
D_FF = 2816
POS_BASE = 10000.0

kernel_name = 'hybrid_diffusion_interleaved_step'


def _layers_of(kind):
    return len(range(kind, DEPTH, N_MIXERS))


def rmsnorm(x, g):
    xf = x.astype(jnp.float32)
    y = xf * lax.rsqrt(jnp.mean(xf * xf, axis=-1, keepdims=True) + RMS_EPS)
    return (y * g.astype(jnp.float32)).astype(x.dtype)


def adaln_params(cvec, w, b):
    m = jax.nn.silu(cvec) @ w + b
    return jnp.split(m[:, None, :], 6, axis=-1)


def modulate(h, shift, scale):
    return h * (1.0 + scale) + shift


def dwconv3(h, w):
    hp = jnp.pad(h, ((0, 0), (1, 1), (0, 0)))
    return hp[:, :-2] * w[0] + hp[:, 1:-1] * w[1] + hp[:, 2:] * w[2]


def grid_pos_embed(n_tokens):
    rows = n_tokens // GRID_W
    rr, cc = jnp.meshgrid(jnp.arange(rows, dtype=jnp.float32), jnp.arange(GRID_W, dtype=jnp.float32), indexing='ij')
    rr, cc = rr.reshape(-1, 1), cc.reshape(-1, 1)
    quarter = D_MODEL // 4
    freq = jnp.exp(-math.log(POS_BASE) * jnp.arange(quarter, dtype=jnp.float32) / quarter)
    return jnp.concatenate([jnp.sin(rr * freq), jnp.cos(rr * freq), jnp.sin(cc * freq), jnp.cos(cc * freq)], axis=-1)


def short_conv_mixer(h, w_in, w_dw, w_out):
    bg, cg, xh = jnp.split(h @ w_in, 3, axis=-1)
    return (bg * dwconv3(cg * xh, w_dw)) @ w_out


def pool_mixer(h, w_group, scale):
    B, T, D = h.shape
    hf = h.astype(jnp.float32)
    csum = jnp.concatenate([jnp.zeros((B, 1, D), jnp.float32), jnp.cumsum(hf, axis=1)], axis=1)
    t = jnp.arange(T)
    outs = []
    for g, w in enumerate(POOL_WINDOWS):
        lo = jnp.maximum(t - w // 2, 0)
        hi = jnp.minimum(t + w // 2, T)
        sl = slice(g * POOL_GROUP, (g + 1) * POOL_GROUP)
        cg = csum[:, :, sl]
        mean = (cg[:, hi] - cg[:, lo]) / (hi - lo).astype(jnp.float32)[None, :, None]
        pooled = (mean - hf[:, :, sl]).astype(h.dtype)
        outs.append(pooled @ w_group[g])
    return jnp.concatenate(outs, axis=-1) * scale


def sgu_mixer(h, w_in, norm_g, w_s, b_s, w_out):
    B, T, D = h.shape
    u, v = jnp.split(jax.nn.gelu(h @ w_in), 2, axis=-1)
    v = rmsnorm(v, norm_g).reshape(B, T // SGU_CHUNK, SGU_CHUNK, SGU_GROUPS, SGU_DIM)
    s = jnp.einsum('gpq,bnqgc->bnpgc', w_s, v) + b_s.T[None, None, :, :, None]
    return (u * s.reshape(B, T, D)) @ w_out


def hgrn_lower_bounds(lb_param):
    p = jax.nn.softmax(lb_param.astype(jnp.float32), axis=0)
    return jnp.cumsum(p, axis=0) - p[0:1]


def hgrn_scan(q, k, v, logf, s0):
    B, T, H, _ = q.shape

    def to_chunks(a):
        return a.reshape(B, T // HGRN_CHUNK, HGRN_CHUNK, H, a.shape[-1]).transpose(1, 0, 3, 2, 4)

    mask = jnp.tril(jnp.ones((HGRN_CHUNK, HGRN_CHUNK), dtype=bool))

    def step(S, xs):
        qc, kc, vc, lfc = xs
        bc = jnp.cumsum(lfc, axis=2)
        o_inter = jnp.einsum('bhtd,bhde->bhte', qc * jnp.exp(bc), S)
        diff = bc[:, :, :, None, :] - bc[:, :, None, :, :]
        decay = jnp.where(mask[:, :, None], jnp.exp(jnp.minimum(diff, 0.0)), 0.0)
        scores = jnp.einsum('bhtd,bhtsd,bhsd->bhts', qc, decay, kc)
        o_intra = jnp.einsum('bhts,bhse->bhte', scores, vc)
        last = bc[:, :, -1:, :]
        S_new = jnp.exp(last[:, :, 0])[..., None] * S + jnp.einsum('bhsd,bhse->bhde', kc * jnp.exp(last - bc), vc)
        return S_new, o_inter + o_intra

    s_final, o = lax.scan(step, s0, (to_chunks(q), to_chunks(k), to_chunks(v), to_chunks(logf)))
    o = o.transpose(1, 0, 3, 2, 4).reshape(B, T, H, -1)
    return o, s_final


def hgrn_mixer(h, w_in, lb, norm_g, w_out, s0):
    B, T, D = h.shape
    q, zf, zb, inp, g = jnp.split(h @ w_in, 5, axis=-1)

    def heads(a):
        return a.reshape(B, T, HGRN_HEADS, -1).astype(jnp.float32)

    def gates(z, lb_d):
        lbh = lb_d.reshape(HGRN_HEADS, HGRN_DK)
        logf = jnp.logaddexp(jnp.log(lbh), jnp.log1p(-lbh) + jax.nn.log_sigmoid(z))
        k = (1.0 - lbh) * jax.nn.sigmoid(-z)
        return logf, k

    qh, ih = heads(q), heads(inp)
    lf_f, k_f = gates(heads(zf), lb[0])
    lf_b, k_b = gates(heads(zb), lb[1])
    s0 = s0.astype(jnp.float32)
    o_f, s_f = hgrn_scan(qh, k_f, ih, lf_f, s0[:, 0])
    o_b, s_b = hgrn_scan(qh[:, ::-1], k_b[:, ::-1], ih[:, ::-1], lf_b[:, ::-1], s0[:, 1])
    o = o_f + o_b[:, ::-1]
    o = rmsnorm(o, norm_g.reshape(HGRN_HEADS, HGRN_DV)).astype(h.dtype).reshape(B, T, D)
    return (o * jax.nn.silu(g)) @ w_out, jnp.stack([s_f, s_b], axis=1)


def conv_ffn(h, w_up, w_dw, w_down):
    a, b = jnp.split(dwconv3(h @ w_up, w_dw), 2, axis=-1)
    return (jax.nn.silu(a) * b) @ w_down


def setup_inputs(seed: int = 0) -> dict:
    key = jax.random.key(seed)
    ks = iter(jax.random.split(key, 32))

    def nrm(shape, scale):
        return scale * jax.random.normal(next(ks), shape, jnp.float32)

    n_a, n_b, n_c, n_d = (_layers_of(k) for k in range(N_MIXERS))
    D = D_MODEL
    return {
        'x_prompt': nrm((BATCH, SEQ, D), 1.0),
        'x_sample': nrm((DEC_BATCH, DEC_SEQ, D), 1.0),
        'state_rec': nrm((DEC_BATCH, n_d, 2, HGRN_HEADS, HGRN_DK, HGRN_DV), 0.5),
        'c': nrm((DEC_BATCH, D), 1.0),
        'c_ctx': nrm((D,), 1.0),
        'ada_w': nrm((DEPTH, D, 6 * D), 0.5 * D ** -0.5),
        'ada_b': nrm((DEPTH, 6 * D), 0.01),
        'norm_g': 1.0 + nrm((DEPTH, 2, D), 0.1),
        'final_g': 1.0 + nrm((D,), 0.1),
        'conv_w_in': nrm((n_a, D, 3 * D), D ** -0.5),
        'conv_w_dw': nrm((n_a, CONV_W, D), CONV_W ** -0.5),
        'conv_w_out': nrm((n_a, D, D), D ** -0.5),
        'pool_w': nrm((n_b, len(POOL_WINDOWS), POOL_GROUP, POOL_GROUP), POOL_GROUP ** -0.5),
        'pool_scale': 1.0 + nrm((n_b, D), 0.1),
        'sgu_w_in': nrm((n_c, D, 2 * D), D ** -0.5),
        'sgu_norm_g': 1.0 + nrm((n_c, D), 0.1),
        'sgu_w_s': nrm((n_c, SGU_GROUPS, SGU_CHUNK, SGU_CHUNK), SGU_CHUNK ** -0.5),
        'sgu_b_s': 1.0 + nrm((n_c, SGU_GROUPS, SGU_CHUNK), 0.1),
        'sgu_w_out': nrm((n_c, D, D), D ** -0.5),
        'hgrn_w_in': nrm((n_d, D, 5 * D), D ** -0.5),
        'hgrn_lb': nrm((DEPTH, 2, D), 0.1),
        'hgrn_norm_g': 1.0 + nrm((n_d, D), 0.1),
        'hgrn_w_out': nrm((n_d, D, D), D ** -0.5),
        'ffn_w_up': nrm((DEPTH, D, 2 * D_FF), D ** -0.5),
        'ffn_w_dw': nrm((DEPTH, CONV_W, 2 * D_FF), CONV_W ** -0.5),
        'ffn_w_down': nrm((DEPTH, D_FF, D), D_FF ** -0.5),
    }


def reference(x_prompt, x_sample, state_rec, c, c_ctx, ada_w, ada_b, norm_g, final_g,
              conv_w_in, conv_w_dw, conv_w_out, pool_w, pool_scale,
              sgu_w_in, sgu_norm_g, sgu_w_s, sgu_b_s, sgu_w_out,
              hgrn_w_in, hgrn_lb, hgrn_norm_g, hgrn_w_out,
              ffn_w_up, ffn_w_dw, ffn_w_down):
    xp = x_prompt
    xs = x_sample + grid_pos_embed(x_sample.shape[1]).astype(x_sample.dtype)[None]
    lbs = hgrn_lower_bounds(hgrn_lb)
    ctx_states = []
    for i in range(DEPTH):
        kind, j = i % N_MIXERS, i // N_MIXERS
        mp = adaln_params(c_ctx[None], ada_w[i], ada_b[i])
        ms = adaln_params(c, ada_w[i], ada_b[i])
        hp = modulate(rmsnorm(xp, norm_g[i, 0]), mp[0], mp[1])
        hs = modulate(rmsnorm(xs, norm_g[i, 0]), ms[0], ms[1])
        if kind == 0:
            yp = short_conv_mixer(hp, conv_w_in[j], conv_w_dw[j], conv_w_out[j])
            ys = short_conv_mixer(hs, conv_w_in[j], conv_w_dw[j], conv_w_out[j])
        elif kind == 1:
            yp = pool_mixer(hp, pool_w[j], pool_scale[j])
            ys = pool_mixer(hs, pool_w[j], pool_scale[j])
        elif kind == 2:
            yp = sgu_mixer(hp, sgu_w_in[j], sgu_norm_g[j], sgu_w_s[j], sgu_b_s[j], sgu_w_out[j])
            ys = sgu_mixer(hs, sgu_w_in[j], sgu_norm_g[j], sgu_w_s[j], sgu_b_s[j], sgu_w_out[j])
        else:
            zero_state = jnp.zeros((xp.shape[0], 2, HGRN_HEADS, HGRN_DK, HGRN_DV), jnp.float32)
            yp, sp = hgrn_mixer(hp, hgrn_w_in[j], lbs[i], hgrn_norm_g[j], hgrn_w_out[j], zero_state)
            ys, _ = hgrn_mixer(hs, hgrn_w_in[j], lbs[i], hgrn_norm_g[j], hgrn_w_out[j], state_rec[:, j])
            ctx_states.append(sp)
        xp = xp + mp[2] * yp
        xs = xs + ms[2] * ys
        hp = modulate(rmsnorm(xp, norm_g[i, 1]), mp[3], mp[4])
        hs = modulate(rmsnorm(xs, norm_g[i, 1]), ms[3], ms[4])
        xp = xp + mp[5] * conv_ffn(hp, ffn_w_up[i], ffn_w_dw[i], ffn_w_down[i])
        xs = xs + ms[5] * conv_ffn(hs, ffn_w_up[i], ffn_w_dw[i], ffn_w_down[i])
    new_state_rec = jnp.stack(ctx_states, axis=1)
    y_prompt = rmsnorm(xp, final_g)
    y_sample = rmsnorm(xs, final_g)
    return (y_prompt, y_sample, new_state_rec)
```

```python
import functools
import math

import jax
import jax.numpy as jnp
from jax import lax
from jax.experimental import pallas as pl
from jax.experimental.pallas import tpu as pltpu

GRID_W = 64
N_MIXERS = 4
RMS_EPS = 1e-6
POOL_WINDOWS = (2, 4, 8, 16)
SGU_CHUNK = 128
POS_BASE = 10000.0

HALO = 8
TB = 512
FFN_FC = 256
HGRN_C = 64
HGRN_TB = 256
EXP_CLAMP = 80.0
VMEM_LIMIT = 60000 * 1024

BF16 = jnp.bfloat16
F32 = jnp.float32


def _cparams(sem):
    return pltpu.CompilerParams(dimension_semantics=sem, vmem_limit_bytes=VMEM_LIMIT)


def _dot(a, b):
    return jnp.dot(a, b, preferred_element_type=F32)


def _const_spec(shape):
    nd = len(shape)
    return pl.BlockSpec(shape, lambda i: (0,) * nd)


class _Layout:
    def __init__(self, batch_p, seq_p, batch_s, seq_s, d):
        self.lp, self.ls, self.d = seq_p, seq_s, d
        self.np_rows = batch_p * seq_p
        self.ns_rows = batch_s * seq_s
        self.nt = self.np_rows + self.ns_rows
        self.batch_p, self.batch_s = batch_p, batch_s

    def check_block(self, tb):
        assert self.np_rows % tb == 0 and self.ns_rows % tb == 0
        for L in (self.lp, self.ls):
            assert L % tb == 0 or tb % L == 0

    def cond_row(self, r0):
        return jnp.where(r0 < self.np_rows, 0, 1 + (jnp.maximum(r0 - self.np_rows, 0)) // self.ls)


def _pos_in_seq(lay, r0, tb):
    k = lax.broadcasted_iota(jnp.int32, (tb, 1), 0)

    def pos_for(start, L):
        base = lax.rem(jnp.maximum(r0 - start, 0), L)
        p = base + k
        for j in range(1, (tb + L - 1) // L + 1):
            p = p - jnp.where(base + k >= j * L, L, 0)
        return p

    is_p = r0 < lay.np_rows
    pos = jnp.where(is_p, pos_for(0, lay.lp), pos_for(lay.np_rows, lay.ls))
    L = jnp.where(is_p, lay.lp, lay.ls)
    return pos, L


def _rms_mod(x, g, shift, scale):
    ms = jnp.mean(x * x, axis=-1, keepdims=True)
    y = x * lax.rsqrt(ms + RMS_EPS) * g
    return y * (1.0 + scale) + shift


def _mod_parts(mod_ref, d, first):
    o = 0 if first else 3 * d
    return (mod_ref[:, o:o + d], mod_ref[:, o + d:o + 2 * d], mod_ref[:, o + 2 * d:o + 3 * d])


def _halo_specs(lay, tb):
    nb8 = lay.nt // HALO
    r = tb // HALO
    return [
        pl.BlockSpec((tb, lay.d), lambda i: (i, 0)),
        pl.BlockSpec((HALO, lay.d), lambda i: (jnp.maximum(i * r - 1, 0), 0)),
        pl.BlockSpec((HALO, lay.d), lambda i: (jnp.minimum((i + 1) * r, nb8 - 1), 0)),
    ]


def _mod_spec(lay, layer, tb):
    return pl.BlockSpec((None, 1, 6 * lay.d), lambda i: (layer * 8 + lay.cond_row(i * tb), 0, 0))


def _shift_rows(a, tb):
    n = a.shape[0]
    prev = pltpu.roll(a, 1, axis=0)[HALO:HALO + tb]
    nxt = pltpu.roll(a, n - 1, axis=0)[HALO:HALO + tb]
    return prev, a[HALO:HALO + tb], nxt


def _dwconv3(a, w, first, last, tb):
    prev, mid, nxt = _shift_rows(a, tb)
    prev = jnp.where(first, 0.0, prev)
    nxt = jnp.where(last, 0.0, nxt)
    return prev * w[0:1] + mid * w[1:2] + nxt * w[2:3]


def _pos_kernel(o_ref, *, tb, d):
    i = pl.program_id(0)
    q = d // 4
    t = (i * tb + lax.broadcasted_iota(jnp.int32, (tb, 1), 0)).astype(F32)
    rr = jnp.floor((t + 0.5) / GRID_W)
    cc = t - rr * GRID_W
    j = lax.broadcasted_iota(jnp.int32, (1, q), 1).astype(F32)
    freq = jnp.exp(-math.log(POS_BASE) * j / q)
    ar = rr * freq
    ac = cc * freq
    o_ref[:, 0:q] = jnp.sin(ar)
    o_ref[:, q:2 * q] = jnp.cos(ar)
    o_ref[:, 2 * q:3 * q] = jnp.sin(ac)
    o_ref[:, 3 * q:4 * q] = jnp.cos(ac)


def _pos_table(n_tokens, d):
    tb = 512
    return pl.pallas_call(
        functools.partial(_pos_kernel, tb=tb, d=d),
        out_shape=jax.ShapeDtypeStruct((n_tokens, d), F32),
        grid=(n_tokens // tb,),
        out_specs=pl.BlockSpec((tb, d), lambda i: (i, 0)),
        compiler_params=_cparams(("parallel",)),
        name="pos_table",
    )()


def _ada_kernel(c_ref, w_ref, b_ref, o_ref):
    c = c_ref[...]
    s = (c * jax.nn.sigmoid(c)).astype(BF16)
    o_ref[...] = _dot(s, w_ref[...].astype(BF16)) + b_ref[...]


def _ada_params(cvec, ada_w, ada_b):
    depth, d, n = ada_w.shape
    tn = 1536
    assert n % tn == 0
    return pl.pallas_call(
        _ada_kernel,
        out_shape=jax.ShapeDtypeStruct((depth, 8, n), F32),
        grid=(depth, n // tn),
        in_specs=[
            pl.BlockSpec((8, d), lambda l, j: (0, 0)),
            pl.BlockSpec((None, d, tn), lambda l, j: (l, 0, j)),
            pl.BlockSpec((None, 1, tn), lambda l, j: (l, 0, j)),
        ],
        out_specs=pl.BlockSpec((None, 8, tn), lambda l, j: (l, 0, j)),
        compiler_params=_cparams(("parallel", "parallel")),
        name="ada_params",
    )(cvec, ada_w, ada_b.reshape(depth, 1, n))


def _embed_kernel(xp_ref, xs_ref, pos_ref, o_ref, *, npb):
    i = pl.program_id(0)

    @pl.when(i < npb)
    def _():
        o_ref[...] = xp_ref[...]

    @pl.when(i >= npb)
    def _():
        o_ref[...] = xs_ref[...] + pos_ref[...]


def _embed(lay, xp, xs, pos):
    tb = TB
    npb = lay.np_rows // tb
    psb = lay.ls // tb
    return pl.pallas_call(
        functools.partial(_embed_kernel, npb=npb),
        out_shape=jax.ShapeDtypeStruct((lay.nt, lay.d), F32),
        grid=(lay.nt // tb,),
        in_specs=[
            pl.BlockSpec((tb, lay.d), lambda i: (jnp.minimum(i, npb - 1), 0)),
            pl.BlockSpec((tb, lay.d), lambda i: (jnp.maximum(i - npb, 0), 0)),
            pl.BlockSpec((tb, lay.d), lambda i: (lax.rem(jnp.maximum(i - npb, 0), psb), 0)),
        ],
        out_specs=pl.BlockSpec((tb, lay.d), lambda i: (i, 0)),
        compiler_params=_cparams(("parallel",)),
        name="embed",
    )(xp, xs, pos)


def _final_kernel(x_ref, g_ref, o_ref):
    x = x_ref[...]
    ms = jnp.mean(x * x, axis=-1, keepdims=True)
    o_ref[...] = x * lax.rsqrt(ms + RMS_EPS) * g_ref[...]


def _final_norm(lay, x, g, row0, rows):
    tb = TB
    b0 = row0 // tb
    return pl.pallas_call(
        _final_kernel,
        out_shape=jax.ShapeDtypeStruct((rows, lay.d), F32),
        grid=(rows // tb,),
        in_specs=[pl.BlockSpec((tb, lay.d), lambda i: (i + b0, 0)), _const_spec((1, lay.d))],
        out_specs=pl.BlockSpec((tb, lay.d), lambda i: (i, 0)),
        compiler_params=_cparams(("parallel",)),
        name="final_norm",
    )(x, g)


def _ffn_kernel(x_ref, xp_ref, xn_ref, mod_ref, g_ref, wup_ref, wdw_ref, wdn_ref, o_ref,
                h_ref, acc_ref, *, lay, tb, nf):
    d = lay.d
    i = pl.program_id(0)
    pos, L = _pos_in_seq(lay, i * tb, tb)
    first, last = pos == 0, pos == L - 1
    shift, scale, gate = _mod_parts(mod_ref, d, first=False)
    xh = jnp.concatenate([xp_ref[...], x_ref[...], xn_ref[...]], axis=0)
    h_ref[...] = _rms_mod(xh, g_ref[...], shift, scale).astype(BF16)
    acc_ref[...] = jnp.zeros_like(acc_ref)

    def body(c, carry):
        h = h_ref[...]
        ua = _dot(h, wup_ref[0, c])
        ub = _dot(h, wup_ref[1, c])
        ca = _dwconv3(ua, wdw_ref[0, c], first, last, tb)
        cb = _dwconv3(ub, wdw_ref[1, c], first, last, tb)
        act = (ca * jax.nn.sigmoid(ca) * cb).astype(BF16)
        acc_ref[...] += _dot(act, wdn_ref[c])
        return carry

    lax.fori_loop(0, nf, body, 0)
    o_ref[...] = x_ref[...] + gate * acc_ref[...]


def _ffn(lay, x, mods, layer, g, wup, wdw, wdn):
    tb = TB
    d = lay.d
    nf = wdn.shape[0]
    return pl.pallas_call(
        functools.partial(_ffn_kernel, lay=lay, tb=tb, nf=nf),
        out_shape=jax.ShapeDtypeStruct((lay.nt, d), F32),
        grid=(lay.nt // tb,),
        in_specs=_halo_specs(lay, tb) + [
            _mod_spec(lay, layer, tb),
            _const_spec((1, d)),
            _const_spec(wup.shape),
            _const_spec(wdw.shape),
            _const_spec(wdn.shape),
        ],
        out_specs=pl.BlockSpec((tb, d), lambda i: (i, 0)),
        scratch_shapes=[pltpu.VMEM((tb + 2 * HALO, d), BF16), pltpu.VMEM((tb, d), F32)],
        compiler_params=_cparams(("parallel",)),
        name="conv_ffn",
    )(x, x, x, mods, g, wup, wdw, wdn)


def _sconv_kernel(x_ref, xp_ref, xn_ref, mod_ref, g_ref, win_ref, wdw_ref, wout_ref, o_ref, *, lay, tb):
    d = lay.d
    i = pl.program_id(0)
    pos, L = _pos_in_seq(lay, i * tb, tb)
    first, last = pos == 0, pos == L - 1
    shift, scale, gate = _mod_parts(mod_ref, d, first=True)
    xh = jnp.concatenate([xp_ref[...], x_ref[...], xn_ref[...]], axis=0)
    h = _rms_mod(xh, g_ref[...], shift, scale).astype(BF16)
    bg = _dot(h, win_ref[:, 0:d])[HALO:HALO + tb]
    p = _dot(h, win_ref[:, d:2 * d]) * _dot(h, win_ref[:, 2 * d:3 * d])
    conv = _dwconv3(p, wdw_ref[...], first, last, tb)
    y = _dot((bg * conv).astype(BF16), wout_ref[...])
    o_ref[...] = x_ref[...] + gate * y


def _sconv_layer(lay, x, mods, layer, g, w_in, w_dw, w_out):
    tb = TB
    d = lay.d
    return pl.pallas_call(
        functools.partial(_sconv_kernel, lay=lay, tb=tb),
        out_shape=jax.ShapeDtypeStruct((lay.nt, d), F32),
        grid=(lay.nt // tb,),
        in_specs=_halo_specs(lay, tb) + [
            _mod_spec(lay, layer, tb),
            _const_spec((1, d)),
            _const_spec(w_in.shape),
            _const_spec(w_dw.shape),
            _const_spec(w_out.shape),
        ],
        out_specs=pl.BlockSpec((tb, d), lambda i: (i, 0)),
        compiler_params=_cparams(("parallel",)),
        name="sconv_mixer",
    )(x, x, x, mods, g, w_in, w_dw, w_out)


def _pool_kernel(x_ref, xp_ref, xn_ref, mod_ref, g_ref, w_ref, sc_ref, o_ref, *, lay, tb):
    d = lay.d
    ng = len(POOL_WINDOWS)
    pg = d // ng
    i = pl.program_id(0)
    pos, L = _pos_in_seq(lay, i * tb, tb)
    shift, scale, gate = _mod_parts(mod_ref, d, first=True)
    xh = jnp.concatenate([xp_ref[...], x_ref[...], xn_ref[...]], axis=0)
    h = _rms_mod(xh, g_ref[...], shift, scale)
    n = tb + 2 * HALO
    outs = []
    for gi, w in enumerate(POOL_WINDOWS):
        assert w // 2 <= HALO
        hg = h[:, gi * pg:(gi + 1) * pg]
        s = jnp.zeros((tb, pg), F32)
        cnt = jnp.zeros((tb, 1), F32)
        for k in range(-(w // 2), w // 2):
            valid = jnp.logical_and(pos + k >= 0, pos + k <= L - 1)
            sh = hg if k == 0 else pltpu.roll(hg, (-k) % n, axis=0)
            s = s + jnp.where(valid, sh[HALO:HALO + tb], 0.0)
            cnt = cnt + valid.astype(F32)
        pooled = (s / cnt - hg[HALO:HALO + tb]).astype(BF16)
        outs.append(_dot(pooled, w_ref[gi]))
    y = jnp.concatenate(outs, axis=-1) * sc_ref[...]
    o_ref[...] = x_ref[...] + gate * y


def _pool_layer(lay, x, mods, layer, g, w, sc):
    tb = TB
    d = lay.d
    return pl.pallas_call(
        functools.partial(_pool_kernel, lay=lay, tb=tb),
        out_shape=jax.ShapeDtypeStruct((lay.nt, d), F32),
        grid=(lay.nt // tb,),
        in_specs=_halo_specs(lay, tb) + [
            _mod_spec(lay, layer, tb),
            _const_spec((1, d)),
            _const_spec(w.shape),
            _const_spec((1, d)),
        ],
        out_specs=pl.BlockSpec((tb, d), lambda i: (i, 0)),
        compiler_params=_cparams(("parallel",)),
        name="pool_mixer",
    )(x, x, x, mods, g, w, sc)


def _sgu_kernel(x_ref, mod_ref, g_ref, win_ref, ng_ref, ws_ref, bs_ref, wout_ref, o_ref, s_ref, *, lay, tb):
    d = lay.d
    groups = ws_ref.shape[0]
    gd = d // groups
    shift, scale, gate = _mod_parts(mod_ref, d, first=True)
    x = x_ref[...]
    h = _rms_mod(x, g_ref[...], shift, scale).astype(BF16)
    u = jax.nn.gelu(_dot(h, win_ref[:, 0:d]), approximate=True)
    v = jax.nn.gelu(_dot(h, win_ref[:, d:2 * d]), approximate=True)
    ms = jnp.mean(v * v, axis=-1, keepdims=True)
    vb = (v * lax.rsqrt(ms + RMS_EPS) * ng_ref[...]).astype(BF16)
    for n in range(tb // SGU_CHUNK):
        for gi in range(groups):
            rows = slice(n * SGU_CHUNK, (n + 1) * SGU_CHUNK)
            cols = slice(gi * gd, (gi + 1) * gd)
            s_ref[rows, cols] = _dot(ws_ref[gi], vb[rows, cols]) + bs_ref[:, gi:gi + 1]
    y = _dot((u * s_ref[...]).astype(BF16), wout_ref[...])
    o_ref[...] = x + gate * y


def _sgu_layer(lay, x, mods, layer, g, w_in, norm_g, w_s, b_st, w_out):
    tb = TB
    d = lay.d
    assert tb % SGU_CHUNK == 0 and lay.lp % SGU_CHUNK == 0 and lay.ls % SGU_CHUNK == 0
    return pl.pallas_call(
        functools.partial(_sgu_kernel, lay=lay, tb=tb),
        out_shape=jax.ShapeDtypeStruct((lay.nt, d), F32),
        grid=(lay.nt // tb,),
        in_specs=[
            pl.BlockSpec((tb, d), lambda i: (i, 0)),
            _mod_spec(lay, layer, tb),
            _const_spec((1, d)),
            _const_spec(w_in.shape),
            _const_spec((1, d)),
            _const_spec(w_s.shape),
            _const_spec(b_st.shape),
            _const_spec(w_out.shape),
        ],
        out_specs=pl.BlockSpec((tb, d), lambda i: (i, 0)),
        scratch_shapes=[pltpu.VMEM((tb, d), F32)],
        compiler_params=_cparams(("parallel",)),
        name="sgu_mixer",
    )(x, mods, g, w_in, norm_g, w_s, b_st, w_out)


def _split3(a):
    hi = a.astype(BF16)
    r1 = a - hi.astype(F32)
    mid = r1.astype(BF16)
    lo = (r1 - mid.astype(F32)).astype(BF16)
    return hi, mid, lo


def _hgrn_proj_kernel(x_ref, mod_ref, g_ref, win_ref, lb_ref,
                      qf_ref, kf_ref, qb_ref, kb_ref, v_ref, sg_ref, cv_ref, *, lay, tb, layer):
    d = lay.d
    c = HGRN_C
    shift, scale, _ = _mod_parts(mod_ref, d, first=True)
    h = _rms_mod(x_ref[...], g_ref[...], shift, scale).astype(BF16)

    lbp = lb_ref[...]
    e = jnp.exp(lbp - jnp.max(lbp, axis=0, keepdims=True))
    p = e / jnp.sum(e, axis=0, keepdims=True)
    lb = jnp.zeros((1, 2 * d), F32)
    for j in range(1, layer + 1):
        lb = lb + p[j:j + 1]
    log_lb = jnp.log(lb)
    log1m_lb = jnp.log1p(-lb)

    q = _dot(h, win_ref[:, 0:d])
    v_ref[...] = _dot(h, win_ref[:, 3 * d:4 * d]).astype(BF16)
    gg = _dot(h, win_ref[:, 4 * d:5 * d])
    sg_ref[...] = gg * jax.nn.sigmoid(gg)

    rt = lax.broadcasted_iota(jnp.int32, (c, c), 0)
    cs = lax.broadcasted_iota(jnp.int32, (c, c), 1)
    tri_f = (cs <= rt).astype(BF16)
    tri_b = (cs >= rt).astype(BF16)
    mid = c // 2

    for di, (q_ref, k_ref, tri) in enumerate(((qf_ref, kf_ref, tri_f), (qb_ref, kb_ref, tri_b))):
        z = _dot(h, win_ref[:, (1 + di) * d:(2 + di) * d])
        llb = log_lb[:, di * d:(di + 1) * d]
        l1m = log1m_lb[:, di * d:(di + 1) * d]
        one_m_lb = 1.0 - lb[:, di * d:(di + 1) * d]
        ez = jnp.exp(-jnp.abs(z))
        logsig = jnp.minimum(z, 0.0) - jnp.log1p(ez)
        bb = l1m + logsig
        logf = jnp.maximum(llb, bb) + jnp.log1p(jnp.exp(-jnp.abs(llb - bb)))
        kk = one_m_lb * jnp.where(z >= 0.0, ez, 1.0) / (1.0 + ez)
        for n in range(tb // c):
            rows = slice(n * c, (n + 1) * c)
            hi, md, lo = _split3(logf[rows])
            b = _dot(tri, hi) + _dot(tri, md) + _dot(tri, lo)
            if di == 0:
                btot = b[c - 1:c]
                bref = b[mid - 1:mid]
            else:
                btot = b[0:1]
                bref = b[mid:mid + 1]
            q_ref[rows, :] = (q[rows] * jnp.exp(jnp.minimum(b - bref, EXP_CLAMP))).astype(BF16)
            k_ref[rows, :] = (kk[rows] * jnp.exp(jnp.minimum(bref - b, EXP_CLAMP))).astype(BF16)
            cv_ref[n, 3 * di:3 * di + 1, :] = jnp.exp(bref)
            cv_ref[n, 3 * di + 1:3 * di + 2, :] = jnp.exp(btot - bref)
            cv_ref[n, 3 * di + 2:3 * di + 3, :] = jnp.exp(btot)
    for n in range(tb // c):
        cv_ref[n, 6:8, :] = jnp.zeros((2, d), F32)


def _hgrn_proj(lay, x, mods, layer, g, w_in, lb2):
    tb = TB
    d = lay.d
    c = HGRN_C
    blk = pl.BlockSpec((tb, d), lambda i: (i, 0))
    act = jax.ShapeDtypeStruct((lay.nt, d), BF16)
    return pl.pallas_call(
        functools.partial(_hgrn_proj_kernel, lay=lay, tb=tb, layer=layer),
        out_shape=[act, act, act, act, act, jax.ShapeDtypeStruct((lay.nt, d), F32),
                   jax.ShapeDtypeStruct((lay.nt // c, 8, d), F32)],
        grid=(lay.nt // tb,),
        in_specs=[blk, _mod_spec(lay, layer, tb), _const_spec((1, d)), _const_spec(w_in.shape),
                  _const_spec(lb2.shape)],
        out_specs=[blk, blk, blk, blk, blk, blk, pl.BlockSpec((tb // c, 8, d), lambda i: (i, 0, 0))],
        compiler_params=_cparams(("parallel",)),
        name="hgrn_proj",
    )(x, mods, g, w_in, lb2)


def _hgrn_scan_kernel(qf_ref, kf_ref, vf_ref, cvf_ref, qb_ref, kb_ref, vb_ref, cvb_ref, s0_ref,
                      of_ref, ob_ref, sfin_ref, st_ref, *, lay, tb, heads):
    d = lay.d
    c = HGRN_C
    dk = d // heads
    i = pl.program_id(0)
    r0 = i * tb
    pos, L = _pos_in_seq(lay, r0, tb)
    del pos
    is_p = r0 < lay.np_rows
    base = jnp.where(is_p, lax.rem(r0, lay.lp), lax.rem(jnp.maximum(r0 - lay.np_rows, 0), lay.ls))

    @pl.when(base == 0)
    def _():
        for di in range(2):
            for hd in range(heads):
                s0 = jnp.where(is_p, 0.0, s0_ref[di, hd])
                st_ref[di, hd] = s0.T

    rt = lax.broadcasted_iota(jnp.int32, (c, c), 0)
    cs = lax.broadcasted_iota(jnp.int32, (c, c), 1)
    nchunk = tb // c
    for di, (q_ref, k_ref, v_ref, cv_ref, o_ref) in enumerate(
            ((qf_ref, kf_ref, vf_ref, cvf_ref, of_ref), (qb_ref, kb_ref, vb_ref, cvb_ref, ob_ref))):
        keep = (cs <= rt) if di == 0 else (cs >= rt)
        order = range(nchunk) if di == 0 else range(nchunk - 1, -1, -1)
        for hd in range(heads):
            cols = slice(hd * dk, (hd + 1) * dk)
            st = st_ref[di, hd]
            for n in order:
                rows = slice(n * c, (n + 1) * c)
                qs = q_ref[rows, cols]
                ks = k_ref[rows, cols]
                vv = v_ref[rows, cols]
                e_ref = cv_ref[n, 3 * di:3 * di + 1, cols]
                e_upd = cv_ref[n, 3 * di + 1:3 * di + 2, cols]
                e_tot = cv_ref[n, 3 * di + 2:3 * di + 3, cols]
                sc = lax.dot_general(qs, ks, (((1,), (1,)), ((), ())), preferred_element_type=F32)
                sc = jnp.where(keep, sc, 0.0).astype(BF16)
                qi = (qs.astype(F32) * e_ref).astype(BF16)
                o = _dot(sc, vv) + lax.dot_general(qi, st.astype(BF16), (((1,), (1,)), ((), ())),
                                                   preferred_element_type=F32)
                o_ref[rows, cols] = o
                ku = (ks.astype(F32) * e_upd).astype(BF16)
                st = st * e_tot + lax.dot_general(vv, ku, (((0,), (0,)), ((), ())),
                                                  preferred_element_type=F32)
            st_ref[di, hd] = st

    @pl.when(base + tb == L)
    def _():
        for di in range(2):
            for hd in range(heads):
                sfin_ref[di, hd] = st_ref[di, hd].T


def _hgrn_scan(lay, qf, kf, qb, kb, v, cv, s0, heads):
    tb = HGRN_TB
    d = lay.d
    c = HGRN_C
    dk = d // heads
    lay.check_block(tb)
    assert tb <= lay.lp and tb <= lay.ls
    npb = lay.np_rows // tb
    bps_p, bps_s = lay.lp // tb, lay.ls // tb
    nseq = lay.batch_p + lay.batch_s

    def mirror(i):
        ip = (i // bps_p) * bps_p + (bps_p - 1 - lax.rem(i, bps_p))
        j = jnp.maximum(i - npb, 0)
        isx = npb + (j // bps_s) * bps_s + (bps_s - 1 - lax.rem(j, bps_s))
        return jnp.where(i < npb, ip, isx)

    def seq_of(i):
        return jnp.where(i < npb, i // bps_p, lay.batch_p + jnp.maximum(i - npb, 0) // bps_s)

    fwd = pl.BlockSpec((tb, d), lambda i: (i, 0))
    bwd = pl.BlockSpec((tb, d), lambda i: (mirror(i), 0))
    cvf = pl.BlockSpec((tb // c, 8, d), lambda i: (i, 0, 0))
    cvb = pl.BlockSpec((tb // c, 8, d), lambda i: (mirror(i), 0, 0))
    st_blk = (None, 2, heads, dk, dk)
    s0_spec = pl.BlockSpec(st_blk, lambda i: (jnp.clip(seq_of(i) - lay.batch_p, 0, lay.batch_s - 1), 0, 0, 0, 0))
    sfin_spec = pl.BlockSpec(st_blk, lambda i: (seq_of(i), 0, 0, 0, 0))
    return pl.pallas_call(
        functools.partial(_hgrn_scan_kernel, lay=lay, tb=tb, heads=heads),
        out_shape=[jax.ShapeDtypeStruct((lay.nt, d), F32), jax.ShapeDtypeStruct((lay.nt, d), F32),
                   jax.ShapeDtypeStruct((nseq, 2, heads, dk, dk), F32)],
        grid=(lay.nt // tb,),
        in_specs=[fwd, fwd, fwd, cvf, bwd, bwd, bwd, cvb, s0_spec],
        out_specs=[fwd, bwd, sfin_spec],
        scratch_shapes=[pltpu.VMEM((2, heads, dk, dk), F32)],
        compiler_params=_cparams(("arbitrary",)),
        name="hgrn_scan",
    )(qf, kf, v, cv, qb, kb, v, cv, s0)


def _hgrn_out_kernel(x_ref, of_ref, ob_ref, sg_ref, mod_ref, ng_ref, wout_ref, o_ref, *, lay, heads):
    d = lay.d
    dk = d // heads
    _, _, gate = _mod_parts(mod_ref, d, first=True)
    o = of_ref[...] + ob_ref[...]
    parts = []
    for hd in range(heads):
        oh = o[:, hd * dk:(hd + 1) * dk]
        ms = jnp.mean(oh * oh, axis=-1, keepdims=True)
        parts.append(oh * lax.rsqrt(ms + RMS_EPS))
    on = jnp.concatenate(parts, axis=-1) * ng_ref[...]
    y = _dot((on * sg_ref[...]).astype(BF16), wout_ref[...])
    o_ref[...] = x_ref[...] + gate * y


def _hgrn_out(lay, x, o_f, o_b, sg, mods, layer, norm_g, w_out, heads):
    tb = TB
    d = lay.d
    blk = pl.BlockSpec((tb, d), lambda i: (i, 0))
    return pl.pallas_call(
        functools.partial(_hgrn_out_kernel, lay=lay, heads=heads),
        out_shape=jax.ShapeDtypeStruct((lay.nt, d), F32),
        grid=(lay.nt // tb,),
        in_specs=[blk, blk, blk, blk, _mod_spec(lay, layer, tb), _const_spec((1, d)), _const_spec(w_out.shape)],
        out_specs=blk,
        compiler_params=_cparams(("parallel",)),
        name="hgrn_out",
    )(x, o_f, o_b, sg, mods, norm_g, w_out)


def kernel(x_prompt, x_sample, state_rec, c, c_ctx, ada_w, ada_b, norm_g, final_g, conv_w_in, conv_w_dw,
           conv_w_out, pool_w, pool_scale, sgu_w_in, sgu_norm_g, sgu_w_s, sgu_b_s, sgu_w_out, hgrn_w_in,
           hgrn_lb, hgrn_norm_g, hgrn_w_out, ffn_w_up, ffn_w_dw, ffn_w_down):
    bp, lp, d = x_prompt.shape
    bs, ls, _ = x_sample.shape
    depth = ada_w.shape[0]
    heads = state_rec.shape[3]
    d_ff = ffn_w_down.shape[1]
    lay = _Layout(bp, lp, bs, ls, d)
    lay.check_block(TB)
    assert c.shape[0] == bs and 1 + bs <= 8 and d_ff % FFN_FC == 0
    nf = d_ff // FFN_FC

    cvec = jnp.concatenate([c_ctx[None], c, jnp.zeros((8 - 1 - bs, d), F32)], axis=0)
    mods = _ada_params(cvec, ada_w, ada_b).reshape(depth * 8, 1, 6 * d)

    pos = _pos_table(ls, d)
    x = _embed(lay, x_prompt.reshape(bp * lp, d), x_sample.reshape(bs * ls, d), pos)

    new_states = []
    for i in range(depth):
        kind, j = i % N_MIXERS, i // N_MIXERS
        g1 = norm_g[i, 0].reshape(1, d)
        g2 = norm_g[i, 1].reshape(1, d)
        if kind == 0:
            x = _sconv_layer(lay, x, mods, i, g1, conv_w_in[j].astype(BF16), conv_w_dw[j],
                             conv_w_out[j].astype(BF16))
        elif kind == 1:
            x = _pool_layer(lay, x, mods, i, g1, pool_w[j].astype(BF16), pool_scale[j].reshape(1, d))
        elif kind == 2:
            x = _sgu_layer(lay, x, mods, i, g1, sgu_w_in[j].astype(BF16), sgu_norm_g[j].reshape(1, d),
                           sgu_w_s[j].astype(BF16), sgu_b_s[j].T, sgu_w_out[j].astype(BF16))
        else:
            qf, kf, qb, kb, v, sg, cv = _hgrn_proj(lay, x, mods, i, g1, hgrn_w_in[j].astype(BF16),
                                                   hgrn_lb.reshape(depth, 2 * d))
            o_f, o_b, sfin = _hgrn_scan(lay, qf, kf, qb, kb, v, cv, state_rec[:, j], heads)
            new_states.append(sfin[:bp])
            x = _hgrn_out(lay, x, o_f, o_b, sg, mods, i, hgrn_norm_g[j].reshape(1, d),
                          hgrn_w_out[j].astype(BF16), heads)
        wup = ffn_w_up[i].reshape(d, 2, nf, FFN_FC).transpose(1, 2, 0, 3).astype(BF16)
        wdw = ffn_w_dw[i].reshape(3, 2, nf, FFN_FC).transpose(1, 2, 0, 3)
        wdn = ffn_w_down[i].reshape(nf, FFN_FC, d).astype(BF16)
        x = _ffn(lay, x, mods, i, g2, wup, wdw, wdn)

    fg = final_g.reshape(1, d)
    y_prompt = _final_norm(lay, x, fg, 0, lay.np_rows).reshape(bp, lp, d)
    y_sample = _final_norm(lay, x, fg, lay.np_rows, lay.ns_rows).reshape(bs, ls, d)
    new_state_rec = jnp.stack(new_states, axis=1)
    return (y_prompt, y_sample, new_state_rec)
```

```python
import functools
import math

import jax
import jax.numpy as jnp
from jax import lax
from jax.experimental import pallas as pl
from jax.experimental.pallas import tpu as pltpu

GRID_W = 64
N_MIXERS = 4
RMS_EPS = 1e-6
POOL_WINDOWS = (2, 4, 8, 16)
SGU_CHUNK = 128
POS_BASE = 10000.0

HALO = 8
TB = 512
FFN_TB = 256
FFN_FC = 256
HGRN_C = 64
HGRN_TB = 256
EXP_CLAMP = 80.0
F32_TINY = 1e-37
VMEM_LIMIT = 60000 * 1024

BF16 = jnp.bfloat16
F32 = jnp.float32


def _cparams(sem):
    return pltpu.CompilerParams(dimension_semantics=sem, vmem_limit_bytes=VMEM_LIMIT)


def _dot(a, b):
    return jnp.dot(a, b, preferred_element_type=F32)


def _const_spec(shape):
    nd = len(shape)
    return pl.BlockSpec(shape, lambda i: (0,) * nd)


class _Layout:
    def __init__(self, batch_p, seq_p, batch_s, seq_s, d):
        self.lp, self.ls, self.d = seq_p, seq_s, d
        self.np_rows = batch_p * seq_p
        self.ns_rows = batch_s * seq_s
        self.nt = self.np_rows + self.ns_rows
        self.batch_p, self.batch_s = batch_p, batch_s

    def check_block(self, tb):
        assert self.np_rows % tb == 0 and self.ns_rows % tb == 0
        for L in (self.lp, self.ls):
            assert L % tb == 0 or tb % L == 0

    def cond_row(self, r0):
        return jnp.where(r0 < self.np_rows, 0, 1 + (jnp.maximum(r0 - self.np_rows, 0)) // self.ls)


def _pos_in_seq(lay, r0, tb):
    k = lax.broadcasted_iota(jnp.int32, (tb, 1), 0)

    def pos_for(start, L):
        base = lax.rem(jnp.maximum(r0 - start, 0), L)
        p = base + k
        for j in range(1, (tb + L - 1) // L + 1):
            p = p - jnp.where(base + k >= j * L, L, 0)
        return p

    is_p = r0 < lay.np_rows
    pos = jnp.where(is_p, pos_for(0, lay.lp), pos_for(lay.np_rows, lay.ls))
    L = jnp.where(is_p, lay.lp, lay.ls)
    return pos, L


def _rms_mod(x, g, shift, scale):
    ms = jnp.mean(x * x, axis=-1, keepdims=True)
    y = x * lax.rsqrt(ms + RMS_EPS) * g
    return y * (1.0 + scale) + shift


def _mod_parts(mod_ref, d, first):
    o = 0 if first else 3 * d
    return (mod_ref[:, o:o + d], mod_ref[:, o + d:o + 2 * d], mod_ref[:, o + 2 * d:o + 3 * d])


def _halo_specs(lay, tb):
    nb8 = lay.nt // HALO
    r = tb // HALO
    return [
        pl.BlockSpec((tb, lay.d), lambda i: (i, 0)),
        pl.BlockSpec((HALO, lay.d), lambda i: (jnp.maximum(i * r - 1, 0), 0)),
        pl.BlockSpec((HALO, lay.d), lambda i: (jnp.minimum((i + 1) * r, nb8 - 1), 0)),
    ]


def _mod_spec(lay, layer, tb):
    return pl.BlockSpec((None, 1, 6 * lay.d), lambda i: (layer * 8 + lay.cond_row(i * tb), 0, 0))


def _edge_groups(lay, tb):
    step = math.gcd(math.gcd(lay.lp, lay.ls), tb)
    assert step % HALO == 0
    firsts = list(range(0, tb, step))
    lasts = [(k - HALO) % tb for k in firsts]
    return firsts, lasts


def _zero_rows(a, mask, groups):
    pieces, cur = [], 0
    for r in sorted(groups):
        if r > cur:
            pieces.append(a[cur:r])
        pieces.append(jnp.where(mask[r:r + HALO], 0.0, a[r:r + HALO]))
        cur = r + HALO
    if cur < a.shape[0]:
        pieces.append(a[cur:])
    return jnp.concatenate(pieces, axis=0)


def _dwconv3(a_ref, w, first, last, tb, edges):
    prev = a_ref[HALO - 1:HALO - 1 + tb, :]
    mid = a_ref[HALO:HALO + tb, :]
    nxt = a_ref[HALO + 1:HALO + 1 + tb, :]
    prev = _zero_rows(prev, first, edges[0])
    nxt = _zero_rows(nxt, last, edges[1])
    return prev * w[0:1] + mid * w[1:2] + nxt * w[2:3]


def _pos_kernel(o_ref, *, tb, d):
    i = pl.program_id(0)
    q = d // 4
    t = (i * tb + lax.broadcasted_iota(jnp.int32, (tb, 1), 0)).astype(F32)
    rr = jnp.floor((t + 0.5) / GRID_W)
    cc = t - rr * GRID_W
    j = lax.broadcasted_iota(jnp.int32, (1, q), 1).astype(F32)
    freq = jnp.exp(-math.log(POS_BASE) * j / q)
    ar = rr * freq
    ac = cc * freq
    o_ref[:, 0:q] = jnp.sin(ar)
    o_ref[:, q:2 * q] = jnp.cos(ar)
    o_ref[:, 2 * q:3 * q] = jnp.sin(ac)
    o_ref[:, 3 * q:4 * q] = jnp.cos(ac)


def _pos_table(n_tokens, d):
    tb = 512
    return pl.pallas_call(
        functools.partial(_pos_kernel, tb=tb, d=d),
        out_shape=jax.ShapeDtypeStruct((n_tokens, d), F32),
        grid=(n_tokens // tb,),
        out_specs=pl.BlockSpec((tb, d), lambda i: (i, 0)),
        compiler_params=_cparams(("parallel",)),
        name="pos_table",
    )()


def _ada_kernel(c_ref, w_ref, b_ref, o_ref):
    c = c_ref[...]
    s = (c * jax.nn.sigmoid(c)).astype(BF16)
    o_ref[...] = _dot(s, w_ref[...].astype(BF16)) + b_ref[...]


def _ada_params(cvec, ada_w, ada_b):
    depth, d, n = ada_w.shape
    tn = 1536
    assert n % tn == 0
    return pl.pallas_call(
        _ada_kernel,
        out_shape=jax.ShapeDtypeStruct((depth, 8, n), F32),
        grid=(depth, n // tn),
        in_specs=[
            pl.BlockSpec((8, d), lambda l, j: (0, 0)),
            pl.BlockSpec((None, d, tn), lambda l, j: (l, 0, j)),
            pl.BlockSpec((None, 1, tn), lambda l, j: (l, 0, j)),
        ],
        out_specs=pl.BlockSpec((None, 8, tn), lambda l, j: (l, 0, j)),
        compiler_params=_cparams(("parallel", "parallel")),
        name="ada_params",
    )(cvec, ada_w, ada_b.reshape(depth, 1, n))


def _embed_kernel(xp_ref, xs_ref, pos_ref, o_ref, *, npb):
    i = pl.program_id(0)

    @pl.when(i < npb)
    def _():
        o_ref[...] = xp_ref[...]

    @pl.when(i >= npb)
    def _():
        o_ref[...] = xs_ref[...] + pos_ref[...]


def _embed(lay, xp, xs, pos):
    tb = TB
    npb = lay.np_rows // tb
    psb = lay.ls // tb
    return pl.pallas_call(
        functools.partial(_embed_kernel, npb=npb),
        out_shape=jax.ShapeDtypeStruct((lay.nt, lay.d), F32),
        grid=(lay.nt // tb,),
        in_specs=[
            pl.BlockSpec((tb, lay.d), lambda i: (jnp.minimum(i, npb - 1), 0)),
            pl.BlockSpec((tb, lay.d), lambda i: (jnp.maximum(i - npb, 0), 0)),
            pl.BlockSpec((tb, lay.d), lambda i: (lax.rem(jnp.maximum(i - npb, 0), psb), 0)),
        ],
        out_specs=pl.BlockSpec((tb, lay.d), lambda i: (i, 0)),
        compiler_params=_cparams(("parallel",)),
        name="embed",
    )(xp, xs, pos)


def _final_kernel(x_ref, g_ref, o_ref):
    x = x_ref[...]
    ms = jnp.mean(x * x, axis=-1, keepdims=True)
    o_ref[...] = x * lax.rsqrt(ms + RMS_EPS) * g_ref[...]


def _final_norm(lay, x, g, row0, rows):
    tb = TB
    b0 = row0 // tb
    return pl.pallas_call(
        _final_kernel,
        out_shape=jax.ShapeDtypeStruct((rows, lay.d), F32),
        grid=(rows // tb,),
        in_specs=[pl.BlockSpec((tb, lay.d), lambda i: (i + b0, 0)), _const_spec((1, lay.d))],
        out_specs=pl.BlockSpec((tb, lay.d), lambda i: (i, 0)),
        compiler_params=_cparams(("parallel",)),
        name="final_norm",
    )(x, g)


def _ffn_kernel(x_ref, xp_ref, xn_ref, mod_ref, g_ref, wup_ref, wdw_ref, wdn_ref, o_ref, h_ref, u_ref,
                *, lay, tb, nf):
    d = lay.d
    i = pl.program_id(0)
    pos, L = _pos_in_seq(lay, i * tb, tb)
    first, last = pos == 0, pos == L - 1
    shift, scale, gate = _mod_parts(mod_ref, d, first=False)
    xh = jnp.concatenate([xp_ref[...], x_ref[...], xn_ref[...]], axis=0)
    h_ref[...] = _rms_mod(xh, g_ref[...], shift, scale).astype(BF16)

    edges = _edge_groups(lay, tb)
    h = h_ref[...]
    def up(c):
        for ab in range(2):
            u_ref[c % 2, ab] = _dot(h, wup_ref[ab, c])

    up(0)
    acc = None
    for c in range(nf):
        if c + 1 < nf:
            up(c + 1)
        ca = _dwconv3(u_ref.at[c % 2, 0], wdw_ref[0, c], first, last, tb, edges)
        cb = _dwconv3(u_ref.at[c % 2, 1], wdw_ref[1, c], first, last, tb, edges)
        act = (ca * jax.nn.sigmoid(ca) * cb).astype(BF16)
        part = _dot(act, wdn_ref[c])
        acc = part if acc is None else acc + part
    o_ref[...] = x_ref[...] + gate * acc


def _ffn(lay, x, mods, layer, g, wup, wdw, wdn):
    tb = FFN_TB
    d = lay.d
    nf = wdn.shape[0]
    lay.check_block(tb)
    return pl.pallas_call(
        functools.partial(_ffn_kernel, lay=lay, tb=tb, nf=nf),
        out_shape=jax.ShapeDtypeStruct((lay.nt, d), F32),
        grid=(lay.nt // tb,),
        in_specs=_halo_specs(lay, tb) + [
            _mod_spec(lay, layer, tb),
            _const_spec((1, d)),
            _const_spec(wup.shape),
            _const_spec(wdw.shape),
            _const_spec(wdn.shape),
        ],
        out_specs=pl.BlockSpec((tb, d), lambda i: (i, 0)),
        scratch_shapes=[pltpu.VMEM((tb + 2 * HALO, d), BF16),
                        pltpu.VMEM((2, 2, tb + 2 * HALO, FFN_FC), F32)],
        compiler_params=_cparams(("parallel",)),
        name="conv_ffn",
    )(x, x, x, mods, g, wup, wdw, wdn)


def _sconv_kernel(x_ref, xp_ref, xn_ref, mod_ref, g_ref, win_ref, wdw_ref, wout_ref, o_ref, p_ref, *, lay, tb):
    d = lay.d
    i = pl.program_id(0)
    pos, L = _pos_in_seq(lay, i * tb, tb)
    first, last = pos == 0, pos == L - 1
    shift, scale, gate = _mod_parts(mod_ref, d, first=True)
    xh = jnp.concatenate([xp_ref[...], x_ref[...], xn_ref[...]], axis=0)
    h = _rms_mod(xh, g_ref[...], shift, scale).astype(BF16)
    bg = _dot(h, win_ref[:, 0:d])[HALO:HALO + tb]
    p_ref[...] = _dot(h, win_ref[:, d:2 * d]) * _dot(h, win_ref[:, 2 * d:3 * d])
    conv = _dwconv3(p_ref, wdw_ref[...], first, last, tb, _edge_groups(lay, tb))
    y = _dot((bg * conv).astype(BF16), wout_ref[...])
    o_ref[...] = x_ref[...] + gate * y


def _sconv_layer(lay, x, mods, layer, g, w_in, w_dw, w_out):
    tb = TB
    d = lay.d
    return pl.pallas_call(
        functools.partial(_sconv_kernel, lay=lay, tb=tb),
        out_shape=jax.ShapeDtypeStruct((lay.nt, d), F32),
        grid=(lay.nt // tb,),
        in_specs=_halo_specs(lay, tb) + [
            _mod_spec(lay, layer, tb),
            _const_spec((1, d)),
            _const_spec(w_in.shape),
            _const_spec(w_dw.shape),
            _const_spec(w_out.shape),
        ],
        out_specs=pl.BlockSpec((tb, d), lambda i: (i, 0)),
        scratch_shapes=[pltpu.VMEM((tb + 2 * HALO, d), F32)],
        compiler_params=_cparams(("parallel",)),
        name="sconv_mixer",
    )(x, x, x, mods, g, w_in, w_dw, w_out)


def _pool_kernel(x_ref, xp_ref, xn_ref, mod_ref, g_ref, w_ref, sc_ref, o_ref, h_ref, *, lay, tb):
    d = lay.d
    ng = len(POOL_WINDOWS)
    pg = d // ng
    i = pl.program_id(0)
    pos, L = _pos_in_seq(lay, i * tb, tb)
    shift, scale, gate = _mod_parts(mod_ref, d, first=True)
    xh = jnp.concatenate([xp_ref[...], x_ref[...], xn_ref[...]], axis=0)
    h_ref[...] = _rms_mod(xh, g_ref[...], shift, scale)
    edges = _edge_groups(lay, tb)
    outs = []
    for gi, w in enumerate(POOL_WINDOWS):
        assert w // 2 <= HALO
        cols = slice(gi * pg, (gi + 1) * pg)
        centre = h_ref[HALO:HALO + tb, cols]
        s = centre
        cnt = jnp.ones((tb, 1), F32)
        for k in range(-(w // 2), w // 2):
            if k == 0:
                continue
            outside = (pos + k < 0) if k < 0 else (pos + k > L - 1)
            sh = h_ref[HALO + k:HALO + k + tb, cols]
            s = s + _zero_rows(sh, outside, edges[0] if k < 0 else edges[1])
            cnt = cnt + jnp.where(outside, 0.0, 1.0)
        pooled = (s / cnt - centre).astype(BF16)
        outs.append(_dot(pooled, w_ref[gi]))
    y = jnp.concatenate(outs, axis=-1) * sc_ref[...]
    o_ref[...] = x_ref[...] + gate * y


def _pool_layer(lay, x, mods, layer, g, w, sc):
    tb = TB
    d = lay.d
    return pl.pallas_call(
        functools.partial(_pool_kernel, lay=lay, tb=tb),
        out_shape=jax.ShapeDtypeStruct((lay.nt, d), F32),
        grid=(lay.nt // tb,),
        in_specs=_halo_specs(lay, tb) + [
            _mod_spec(lay, layer, tb),
            _const_spec((1, d)),
            _const_spec(w.shape),
            _const_spec((1, d)),
        ],
        out_specs=pl.BlockSpec((tb, d), lambda i: (i, 0)),
        scratch_shapes=[pltpu.VMEM((tb + 2 * HALO, d), F32)],
        compiler_params=_cparams(("parallel",)),
        name="pool_mixer",
    )(x, x, x, mods, g, w, sc)


def _sgu_kernel(x_ref, mod_ref, g_ref, win_ref, ng_ref, ws_ref, bs_ref, wout_ref, o_ref, s_ref, *, lay, tb):
    d = lay.d
    groups = ws_ref.shape[0]
    gd = d // groups
    shift, scale, gate = _mod_parts(mod_ref, d, first=True)
    x = x_ref[...]
    h = _rms_mod(x, g_ref[...], shift, scale).astype(BF16)
    u = jax.nn.gelu(_dot(h, win_ref[:, 0:d]), approximate=True)
    v = jax.nn.gelu(_dot(h, win_ref[:, d:2 * d]), approximate=True)
    ms = jnp.mean(v * v, axis=-1, keepdims=True)
    vb = (v * lax.rsqrt(ms + RMS_EPS) * ng_ref[...]).astype(BF16)
    for n in range(tb // SGU_CHUNK):
        for gi in range(groups):
            rows = slice(n * SGU_CHUNK, (n + 1) * SGU_CHUNK)
            cols = slice(gi * gd, (gi + 1) * gd)
            s_ref[rows, cols] = _dot(ws_ref[gi], vb[rows, cols]) + bs_ref[:, gi:gi + 1]
    y = _dot((u * s_ref[...]).astype(BF16), wout_ref[...])
    o_ref[...] = x + gate * y


def _sgu_layer(lay, x, mods, layer, g, w_in, norm_g, w_s, b_st, w_out):
    tb = TB
    d = lay.d
    assert tb % SGU_CHUNK == 0 and lay.lp % SGU_CHUNK == 0 and lay.ls % SGU_CHUNK == 0
    return pl.pallas_call(
        functools.partial(_sgu_kernel, lay=lay, tb=tb),
        out_shape=jax.ShapeDtypeStruct((lay.nt, d), F32),
        grid=(lay.nt // tb,),
        in_specs=[
            pl.BlockSpec((tb, d), lambda i: (i, 0)),
            _mod_spec(lay, layer, tb),
            _const_spec((1, d)),
            _const_spec(w_in.shape),
            _const_spec((1, d)),
            _const_spec(w_s.shape),
            _const_spec(b_st.shape),
            _const_spec(w_out.shape),
        ],
        out_specs=pl.BlockSpec((tb, d), lambda i: (i, 0)),
        scratch_shapes=[pltpu.VMEM((tb, d), F32)],
        compiler_params=_cparams(("parallel",)),
        name="sgu_mixer",
    )(x, mods, g, w_in, norm_g, w_s, b_st, w_out)


def _split2(a):
    hi = a.astype(BF16)
    lo = (a - hi.astype(F32)).astype(BF16)
    return hi, lo


def _hgrn_gates(z, lb, one_m_lb):
    ez = jnp.exp(-jnp.abs(z))
    r = 1.0 / (1.0 + ez)
    er = ez * r
    pos = z >= 0.0
    f = lb + one_m_lb * jnp.where(pos, r, er)
    logf = jnp.log(jnp.maximum(f, F32_TINY))
    return logf, one_m_lb * jnp.where(pos, er, r)


def _hgrn_proj_kernel(x_ref, mod_ref, g_ref, win_ref, lb_ref,
                      qf_ref, kf_ref, qb_ref, kb_ref, v_ref, sg_ref, cv_ref, *, lay, tb, layer):
    d = lay.d
    c = HGRN_C
    shift, scale, _ = _mod_parts(mod_ref, d, first=True)
    h = _rms_mod(x_ref[...], g_ref[...], shift, scale).astype(BF16)

    lbp = lb_ref[...]
    e = jnp.exp(lbp - jnp.max(lbp, axis=0, keepdims=True))
    p = e / jnp.sum(e, axis=0, keepdims=True)
    lb = jnp.zeros((1, 2 * d), F32)
    for j in range(1, layer + 1):
        lb = lb + p[j:j + 1]
    one_m_lb = 1.0 - lb

    z = [_dot(h, win_ref[:, d:2 * d]), None]
    q = _dot(h, win_ref[:, 0:d])
    z[1] = _dot(h, win_ref[:, 2 * d:3 * d])
    vv = _dot(h, win_ref[:, 3 * d:4 * d])
    gg = _dot(h, win_ref[:, 4 * d:5 * d])
    gates = [_hgrn_gates(z[di], lb[:, di * d:(di + 1) * d], one_m_lb[:, di * d:(di + 1) * d]) for di in range(2)]
    v_ref[...] = vv.astype(BF16)
    sg_ref[...] = gg * jax.nn.sigmoid(gg)

    rt = lax.broadcasted_iota(jnp.int32, (c, c), 0)
    cs = lax.broadcasted_iota(jnp.int32, (c, c), 1)
    tri_f = (cs <= rt).astype(BF16)
    tri_b = (cs >= rt).astype(BF16)
    mid = c // 2

    for di, (q_ref, k_ref, tri) in enumerate(((qf_ref, kf_ref, tri_f), (qb_ref, kb_ref, tri_b))):
        logf, kk = gates[di]
        for n in range(tb // c):
            rows = slice(n * c, (n + 1) * c)
            hi, lo = _split2(logf[rows])
            b = _dot(tri, hi) + _dot(tri, lo)
            if di == 0:
                btot = b[c - 1:c]
                bref = b[mid - 1:mid]
            else:
                btot = b[0:1]
                bref = b[mid:mid + 1]
            q_ref[rows, :] = (q[rows] * jnp.exp(jnp.minimum(b - bref, EXP_CLAMP))).astype(BF16)
            k_ref[rows, :] = (kk[rows] * jnp.exp(jnp.minimum(bref - b, EXP_CLAMP))).astype(BF16)
            cv_ref[n, 3 * di:3 * di + 1, :] = jnp.exp(bref)
            cv_ref[n, 3 * di + 1:3 * di + 2, :] = jnp.exp(btot - bref)
            cv_ref[n, 3 * di + 2:3 * di + 3, :] = jnp.exp(btot)
    for n in range(tb // c):
        cv_ref[n, 6:8, :] = jnp.zeros((2, d), F32)


def _hgrn_proj(lay, x, mods, layer, g, w_in, lb2):
    tb = TB
    d = lay.d
    c = HGRN_C
    blk = pl.BlockSpec((tb, d), lambda i: (i, 0))
    act = jax.ShapeDtypeStruct((lay.nt, d), BF16)
    return pl.pallas_call(
        functools.partial(_hgrn_proj_kernel, lay=lay, tb=tb, layer=layer),
        out_shape=[act, act, act, act, act, jax.ShapeDtypeStruct((lay.nt, d), F32),
                   jax.ShapeDtypeStruct((lay.nt // c, 8, d), F32)],
        grid=(lay.nt // tb,),
        in_specs=[blk, _mod_spec(lay, layer, tb), _const_spec((1, d)), _const_spec(w_in.shape),
                  _const_spec(lb2.shape)],
        out_specs=[blk, blk, blk, blk, blk, blk, pl.BlockSpec((tb // c, 8, d), lambda i: (i, 0, 0))],
        compiler_params=_cparams(("parallel",)),
        name="hgrn_proj",
    )(x, mods, g, w_in, lb2)


def _hgrn_scan_kernel(qf_ref, kf_ref, vf_ref, cvf_ref, qb_ref, kb_ref, vb_ref, cvb_ref, s0_ref,
                      of_ref, ob_ref, sfin_ref, st_ref, *, lay, tb, heads):
    d = lay.d
    c = HGRN_C
    dk = d // heads
    i = pl.program_id(0)
    r0 = i * tb
    pos, L = _pos_in_seq(lay, r0, tb)
    del pos
    is_p = r0 < lay.np_rows
    base = jnp.where(is_p, lax.rem(r0, lay.lp), lax.rem(jnp.maximum(r0 - lay.np_rows, 0), lay.ls))

    @pl.when(base == 0)
    def _():
        for di in range(2):
            for hd in range(heads):
                s0 = jnp.where(is_p, 0.0, s0_ref[di, hd])
                st_ref[di, hd] = s0.T

    rt = lax.broadcasted_iota(jnp.int32, (c, c), 0)
    cs = lax.broadcasted_iota(jnp.int32, (c, c), 1)
    keep = (cs <= rt, cs >= rt)
    nchunk = tb // c
    dirs = ((qf_ref, kf_ref, vf_ref, cvf_ref, of_ref), (qb_ref, kb_ref, vb_ref, cvb_ref, ob_ref))
    nt_dims = (((1,), (1,)), ((), ()))
    tn_dims = (((0,), (0,)), ((), ()))
    st = [[st_ref[di, hd] for hd in range(heads)] for di in range(2)]
    for n in range(nchunk):
        part = {}
        for di, (q_ref, k_ref, v_ref, cv_ref, _) in enumerate(dirs):
            nn = n if di == 0 else nchunk - 1 - n
            rows = slice(nn * c, (nn + 1) * c)
            for hd in range(heads):
                cols = slice(hd * dk, (hd + 1) * dk)
                qs, ks, vv = q_ref[rows, cols], k_ref[rows, cols], v_ref[rows, cols]
                ku = (ks.astype(F32) * cv_ref[nn, 3 * di + 1:3 * di + 2, cols]).astype(BF16)
                sc = lax.dot_general(qs, ks, nt_dims, preferred_element_type=F32)
                upd = lax.dot_general(vv, ku, tn_dims, preferred_element_type=F32)
                part[di, hd] = (qs, vv, sc, upd)
        for di, (_, _, _, cv_ref, o_ref) in enumerate(dirs):
            nn = n if di == 0 else nchunk - 1 - n
            rows = slice(nn * c, (nn + 1) * c)
            for hd in range(heads):
                cols = slice(hd * dk, (hd + 1) * dk)
                qs, vv, sc, upd = part[di, hd]
                scb = jnp.where(keep[di], sc, 0.0).astype(BF16)
                qi = (qs.astype(F32) * cv_ref[nn, 3 * di:3 * di + 1, cols]).astype(BF16)
                o_ref[rows, cols] = _dot(scb, vv) + lax.dot_general(
                    qi, st[di][hd].astype(BF16), nt_dims, preferred_element_type=F32)
                st[di][hd] = st[di][hd] * cv_ref[nn, 3 * di + 2:3 * di + 3, cols] + upd
    for di in range(2):
        for hd in range(heads):
            st_ref[di, hd] = st[di][hd]

    @pl.when(base + tb == L)
    def _():
        for di in range(2):
            for hd in range(heads):
                sfin_ref[di, hd] = st_ref[di, hd].T


def _hgrn_scan(lay, qf, kf, qb, kb, v, cv, s0, heads):
    tb = HGRN_TB
    d = lay.d
    c = HGRN_C
    dk = d // heads
    lay.check_block(tb)
    assert tb <= lay.lp and tb <= lay.ls
    npb = lay.np_rows // tb
    bps_p, bps_s = lay.lp // tb, lay.ls // tb
    nseq = lay.batch_p + lay.batch_s

    def mirror(i):
        ip = (i // bps_p) * bps_p + (bps_p - 1 - lax.rem(i, bps_p))
        j = jnp.maximum(i - npb, 0)
        isx = npb + (j // bps_s) * bps_s + (bps_s - 1 - lax.rem(j, bps_s))
        return jnp.where(i < npb, ip, isx)

    def seq_of(i):
        return jnp.where(i < npb, i // bps_p, lay.batch_p + jnp.maximum(i - npb, 0) // bps_s)

    fwd = pl.BlockSpec((tb, d), lambda i: (i, 0))
    bwd = pl.BlockSpec((tb, d), lambda i: (mirror(i), 0))
    cvf = pl.BlockSpec((tb // c, 8, d), lambda i: (i, 0, 0))
    cvb = pl.BlockSpec((tb // c, 8, d), lambda i: (mirror(i), 0, 0))
    st_blk = (None, 2, heads, dk, dk)
    s0_spec = pl.BlockSpec(st_blk, lambda i: (jnp.clip(seq_of(i) - lay.batch_p, 0, lay.batch_s - 1), 0, 0, 0, 0))
    sfin_spec = pl.BlockSpec(st_blk, lambda i: (seq_of(i), 0, 0, 0, 0))
    return pl.pallas_call(
        functools.partial(_hgrn_scan_kernel, lay=lay, tb=tb, heads=heads),
        out_shape=[jax.ShapeDtypeStruct((lay.nt, d), F32), jax.ShapeDtypeStruct((lay.nt, d), F32),
                   jax.ShapeDtypeStruct((nseq, 2, heads, dk, dk), F32)],
        grid=(lay.nt // tb,),
        in_specs=[fwd, fwd, fwd, cvf, bwd, bwd, bwd, cvb, s0_spec],
        out_specs=[fwd, bwd, sfin_spec],
        scratch_shapes=[pltpu.VMEM((2, heads, dk, dk), F32)],
        compiler_params=_cparams(("arbitrary",)),
        name="hgrn_scan",
    )(qf, kf, v, cv, qb, kb, v, cv, s0)


def _hgrn_out_kernel(x_ref, of_ref, ob_ref, sg_ref, mod_ref, ng_ref, wout_ref, o_ref, *, lay, heads):
    d = lay.d
    dk = d // heads
    _, _, gate = _mod_parts(mod_ref, d, first=True)
    o = of_ref[...] + ob_ref[...]
    parts = []
    for hd in range(heads):
        oh = o[:, hd * dk:(hd + 1) * dk]
        ms = jnp.mean(oh * oh, axis=-1, keepdims=True)
        parts.append(oh * lax.rsqrt(ms + RMS_EPS))
    on = jnp.concatenate(parts, axis=-1) * ng_ref[...]
    y = _dot((on * sg_ref[...]).astype(BF16), wout_ref[...])
    o_ref[...] = x_ref[...] + gate * y


def _hgrn_out(lay, x, o_f, o_b, sg, mods, layer, norm_g, w_out, heads):
    tb = TB
    d = lay.d
    blk = pl.BlockSpec((tb, d), lambda i: (i, 0))
    return pl.pallas_call(
        functools.partial(_hgrn_out_kernel, lay=lay, heads=heads),
        out_shape=jax.ShapeDtypeStruct((lay.nt, d), F32),
        grid=(lay.nt // tb,),
        in_specs=[blk, blk, blk, blk, _mod_spec(lay, layer, tb), _const_spec((1, d)), _const_spec(w_out.shape)],
        out_specs=blk,
        compiler_params=_cparams(("parallel",)),
        name="hgrn_out",
    )(x, o_f, o_b, sg, mods, norm_g, w_out)


def kernel(x_prompt, x_sample, state_rec, c, c_ctx, ada_w, ada_b, norm_g, final_g, conv_w_in, conv_w_dw,
           conv_w_out, pool_w, pool_scale, sgu_w_in, sgu_norm_g, sgu_w_s, sgu_b_s, sgu_w_out, hgrn_w_in,
           hgrn_lb, hgrn_norm_g, hgrn_w_out, ffn_w_up, ffn_w_dw, ffn_w_down):
    bp, lp, d = x_prompt.shape
    bs, ls, _ = x_sample.shape
    depth = ada_w.shape[0]
    heads = state_rec.shape[3]
    d_ff = ffn_w_down.shape[1]
    lay = _Layout(bp, lp, bs, ls, d)
    lay.check_block(TB)
    assert c.shape[0] == bs and 1 + bs <= 8 and d_ff % FFN_FC == 0
    nf = d_ff // FFN_FC

    cvec = jnp.concatenate([c_ctx[None], c, jnp.zeros((8 - 1 - bs, d), F32)], axis=0)
    mods = _ada_params(cvec, ada_w, ada_b).reshape(depth * 8, 1, 6 * d)

    pos = _pos_table(ls, d)
    x = _embed(lay, x_prompt.reshape(bp * lp, d), x_sample.reshape(bs * ls, d), pos)

    new_states = []
    for i in range(depth):
        kind, j = i % N_MIXERS, i // N_MIXERS
        g1 = norm_g[i, 0].reshape(1, d)
        g2 = norm_g[i, 1].reshape(1, d)
        if kind == 0:
            x = _sconv_layer(lay, x, mods, i, g1, conv_w_in[j].astype(BF16), conv_w_dw[j],
                             conv_w_out[j].astype(BF16))
        elif kind == 1:
            x = _pool_layer(lay, x, mods, i, g1, pool_w[j].astype(BF16), pool_scale[j].reshape(1, d))
        elif kind == 2:
            x = _sgu_layer(lay, x, mods, i, g1, sgu_w_in[j].astype(BF16), sgu_norm_g[j].reshape(1, d),
                           sgu_w_s[j].astype(BF16), sgu_b_s[j].T, sgu_w_out[j].astype(BF16))
        else:
            qf, kf, qb, kb, v, sg, cv = _hgrn_proj(lay, x, mods, i, g1, hgrn_w_in[j].astype(BF16),
                                                   hgrn_lb.reshape(depth, 2 * d))
            o_f, o_b, sfin = _hgrn_scan(lay, qf, kf, qb, kb, v, cv, state_rec[:, j], heads)
            new_states.append(sfin[:bp])
            x = _hgrn_out(lay, x, o_f, o_b, sg, mods, i, hgrn_norm_g[j].reshape(1, d),
                          hgrn_w_out[j].astype(BF16), heads)
        wup = ffn_w_up[i].reshape(d, 2, nf, FFN_FC).transpose(1, 2, 0, 3).astype(BF16)
        wdw = ffn_w_dw[i].reshape(3, 2, nf, FFN_FC).transpose(1, 2, 0, 3)
        wdn = ffn_w_down[i].reshape(nf, FFN_FC, d).astype(BF16)
        x = _ffn(lay, x, mods, i, g2, wup, wdw, wdn)

    fg = final_g.reshape(1, d)
    y_prompt = _final_norm(lay, x, fg, 0, lay.np_rows).reshape(bp, lp, d)
    y_sample = _final_norm(lay, x, fg, lay.np_rows, lay.ns_rows).reshape(bs, ls, d)
    new_state_rec = jnp.stack(new_states, axis=1)
    return (y_prompt, y_sample, new_state_rec)
```

```python
import functools
import math

import jax
import jax.numpy as jnp
from jax import lax
from jax.experimental import pallas as pl
from jax.experimental.pallas import tpu as pltpu

GRID_W = 64
N_MIXERS = 4
RMS_EPS = 1e-6
POOL_WINDOWS = (2, 4, 8, 16)
SGU_CHUNK = 128
POS_BASE = 10000.0

HALO = 8
TB = 512
FFN_TB = 256
FFN_FC = 256
HGRN_C = 64
HGRN_TB = 256
EXP_CLAMP = 80.0
F32_TINY = 1e-37
VMEM_LIMIT = 60000 * 1024

BF16 = jnp.bfloat16
F32 = jnp.float32


def _cparams(sem):
    return pltpu.CompilerParams(dimension_semantics=sem, vmem_limit_bytes=VMEM_LIMIT)


def _dot(a, b):
    return jnp.dot(a, b, preferred_element_type=F32)


def _const_spec(shape):
    nd = len(shape)
    return pl.BlockSpec(shape, lambda i: (0,) * nd)


class _Layout:
    def __init__(self, batch_p, seq_p, batch_s, seq_s, d):
        self.lp, self.ls, self.d = seq_p, seq_s, d
        self.np_rows = batch_p * seq_p
        self.ns_rows = batch_s * seq_s
        self.nt = self.np_rows + self.ns_rows
        self.batch_p, self.batch_s = batch_p, batch_s

    def check_block(self, tb):
        assert self.np_rows % tb == 0 and self.ns_rows % tb == 0
        for L in (self.lp, self.ls):
            assert L % tb == 0 or tb % L == 0

    def cond_row(self, r0):
        return jnp.where(r0 < self.np_rows, 0, 1 + (jnp.maximum(r0 - self.np_rows, 0)) // self.ls)


def _pos_in_seq(lay, r0, tb):
    k = lax.broadcasted_iota(jnp.int32, (tb, 1), 0)

    def pos_for(start, L):
        base = lax.rem(jnp.maximum(r0 - start, 0), L)
        p = base + k
        for j in range(1, (tb + L - 1) // L + 1):
            p = p - jnp.where(base + k >= j * L, L, 0)
        return p

    is_p = r0 < lay.np_rows
    pos = jnp.where(is_p, pos_for(0, lay.lp), pos_for(lay.np_rows, lay.ls))
    L = jnp.where(is_p, lay.lp, lay.ls)
    return pos, L


def _rms_mod(x, g, shift, scale):
    ms = jnp.mean(x * x, axis=-1, keepdims=True)
    y = x * lax.rsqrt(ms + RMS_EPS) * g
    return y * (1.0 + scale) + shift


def _mod_parts(mod_ref, d, first):
    o = 0 if first else 3 * d
    return (mod_ref[:, o:o + d], mod_ref[:, o + d:o + 2 * d], mod_ref[:, o + 2 * d:o + 3 * d])


def _halo_specs(lay, tb):
    nb8 = lay.nt // HALO
    r = tb // HALO
    return [
        pl.BlockSpec((tb, lay.d), lambda i: (i, 0)),
        pl.BlockSpec((HALO, lay.d), lambda i: (jnp.maximum(i * r - 1, 0), 0)),
        pl.BlockSpec((HALO, lay.d), lambda i: (jnp.minimum((i + 1) * r, nb8 - 1), 0)),
    ]


def _mod_spec(lay, layer, tb):
    return pl.BlockSpec((None, 1, 6 * lay.d), lambda i: (layer * 8 + lay.cond_row(i * tb), 0, 0))


def _edge_groups(lay, tb):
    step = math.gcd(math.gcd(lay.lp, lay.ls), tb)
    assert step % HALO == 0
    firsts = list(range(0, tb, step))
    lasts = [(k - HALO) % tb for k in firsts]
    return firsts, lasts


def _zero_rows(a, mask, groups):
    pieces, cur = [], 0
    for r in sorted(groups):
        if r > cur:
            pieces.append(a[cur:r])
        pieces.append(jnp.where(mask[r:r + HALO], 0.0, a[r:r + HALO]))
        cur = r + HALO
    if cur < a.shape[0]:
        pieces.append(a[cur:])
    return jnp.concatenate(pieces, axis=0)


def _dwconv3(a_ref, w, first, last, tb, edges):
    prev = a_ref[HALO - 1:HALO - 1 + tb, :]
    mid = a_ref[HALO:HALO + tb, :]
    nxt = a_ref[HALO + 1:HALO + 1 + tb, :]
    prev = _zero_rows(prev, first, edges[0])
    nxt = _zero_rows(nxt, last, edges[1])
    return prev * w[0:1] + mid * w[1:2] + nxt * w[2:3]


def _pos_kernel(o_ref, *, tb, d):
    i = pl.program_id(0)
    q = d // 4
    t = (i * tb + lax.broadcasted_iota(jnp.int32, (tb, 1), 0)).astype(F32)
    rr = jnp.floor((t + 0.5) / GRID_W)
    cc = t - rr * GRID_W
    j = lax.broadcasted_iota(jnp.int32, (1, q), 1).astype(F32)
    freq = jnp.exp(-math.log(POS_BASE) * j / q)
    ar = rr * freq
    ac = cc * freq
    o_ref[:, 0:q] = jnp.sin(ar)
    o_ref[:, q:2 * q] = jnp.cos(ar)
    o_ref[:, 2 * q:3 * q] = jnp.sin(ac)
    o_ref[:, 3 * q:4 * q] = jnp.cos(ac)


def _pos_table(n_tokens, d):
    tb = 512
    return pl.pallas_call(
        functools.partial(_pos_kernel, tb=tb, d=d),
        out_shape=jax.ShapeDtypeStruct((n_tokens, d), F32),
        grid=(n_tokens // tb,),
        out_specs=pl.BlockSpec((tb, d), lambda i: (i, 0)),
        compiler_params=_cparams(("parallel",)),
        name="pos_table",
    )()


def _ada_kernel(c_ref, w_ref, b_ref, o_ref):
    c = c_ref[...]
    s = (c * jax.nn.sigmoid(c)).astype(BF16)
    o_ref[...] = _dot(s, w_ref[...].astype(BF16)) + b_ref[...]


def _ada_params(cvec, ada_w, ada_b):
    depth, d, n = ada_w.shape
    tn = 1536
    assert n % tn == 0
    return pl.pallas_call(
        _ada_kernel,
        out_shape=jax.ShapeDtypeStruct((depth, 8, n), F32),
        grid=(depth, n // tn),
        in_specs=[
            pl.BlockSpec((8, d), lambda l, j: (0, 0)),
            pl.BlockSpec((None, d, tn), lambda l, j: (l, 0, j)),
            pl.BlockSpec((None, 1, tn), lambda l, j: (l, 0, j)),
        ],
        out_specs=pl.BlockSpec((None, 8, tn), lambda l, j: (l, 0, j)),
        compiler_params=_cparams(("parallel", "parallel")),
        name="ada_params",
    )(cvec, ada_w, ada_b.reshape(depth, 1, n))


def _edge_points(lay, r0, tb):
    is_p = r0 < lay.np_rows
    base = jnp.where(is_p, lax.rem(r0, lay.lp), lax.rem(jnp.maximum(r0 - lay.np_rows, 0), lay.ls))
    L = jnp.where(is_p, lay.lp, lay.ls)
    step = math.gcd(math.gcd(lay.lp, lay.ls), tb)
    firsts = [(k, lax.rem(base + k, L) == 0) for k in range(0, tb, step)]
    lasts = [((k - 1) % tb, lax.rem(base + ((k - 1) % tb) + 1, L) == 0) for k in range(0, tb, step)]
    return firsts, lasts


def _zero_points(a, points, s_rows):
    sub = lax.broadcasted_iota(jnp.int32, (HALO, 1), 0)
    pieces, cur = [], 0
    for k, flag in sorted(points, key=lambda p: (p[0] + HALO) % s_rows):
        s, j = divmod(k + HALO, s_rows)
        r = HALO * j
        assert r >= cur
        if r > cur:
            pieces.append(a[cur:r])
        pieces.append(jnp.where(jnp.logical_and(sub == s, flag), 0.0, a[r:r + HALO]))
        cur = r + HALO
    if cur < a.shape[0]:
        pieces.append(a[cur:])
    return jnp.concatenate(pieces, axis=0)


def _dwconv3_strided(u_ref, w, firsts, lasts):
    n = u_ref.shape[0]
    s_rows = n // HALO
    mid = u_ref[...]
    prev = jnp.concatenate([pltpu.roll(u_ref[n - HALO:n, :], 1, axis=0), u_ref[0:n - HALO, :]], axis=0)
    nxt = jnp.concatenate([u_ref[HALO:n, :], pltpu.roll(u_ref[0:HALO, :], HALO - 1, axis=0)], axis=0)
    prev = _zero_points(prev, firsts, s_rows)
    nxt = _zero_points(nxt, lasts, s_rows)
    return prev * w[0:1] + mid * w[1:2] + nxt * w[2:3]


def _ffn_kernel(*refs, lay, tb, nf, final):
    if final:
        (x_ref, xp_ref, xn_ref, mod_ref, g_ref, wup_ref, wdw_ref, wdn_ref, fg_ref,
         yp_ref, ys_ref, slab_ref, u_ref) = refs
    else:
        x_ref, xp_ref, xn_ref, mod_ref, g_ref, wup_ref, wdw_ref, wdn_ref, o_ref, slab_ref, u_ref = refs
    d = lay.d
    n = tb + 2 * HALO
    s_rows = n // HALO
    nl = d // 128
    i = pl.program_id(0)
    firsts, lasts = _edge_points(lay, i * tb, tb)
    shift, scale, gate = _mod_parts(mod_ref, d, first=False)

    for l in range(nl):
        cols = slice(128 * l, 128 * (l + 1))
        slab_ref[l, 0:HALO, :] = xp_ref[:, cols]
        slab_ref[l, HALO:HALO + tb, :] = x_ref[:, cols]
        slab_ref[l, HALO + tb:n, :] = xn_ref[:, cols]
    xs = jnp.concatenate(
        [jnp.concatenate([slab_ref[l, pl.ds(j, HALO, stride=s_rows), :] for l in range(nl)], axis=1)
         for j in range(s_rows)], axis=0)
    h = _rms_mod(xs, g_ref[...], shift, scale).astype(BF16)

    def up(c):
        for ab in range(2):
            u_ref[c % 2, ab] = _dot(h, wup_ref[ab, c])

    up(0)
    acc = None
    for c in range(nf):
        if c + 1 < nf:
            up(c + 1)
        ca = _dwconv3_strided(u_ref.at[c % 2, 0], wdw_ref[0, c], firsts, lasts)
        cb = _dwconv3_strided(u_ref.at[c % 2, 1], wdw_ref[1, c], firsts, lasts)
        act = (ca * jax.nn.sigmoid(ca) * cb).astype(BF16)
        part = _dot(act, wdn_ref[c])
        acc = part if acc is None else acc + part

    for l in range(nl):
        for j in range(s_rows):
            slab_ref[l, pl.ds(j, HALO, stride=s_rows), :] = acc[HALO * j:HALO * (j + 1), 128 * l:128 * (l + 1)]
    y = jnp.concatenate([slab_ref[l, HALO:HALO + tb, :] for l in range(nl)], axis=1)
    out = x_ref[...] + gate * y
    if not final:
        o_ref[...] = out
    else:
        ms = jnp.mean(out * out, axis=-1, keepdims=True)
        out = out * lax.rsqrt(ms + RMS_EPS) * fg_ref[...]
        is_p = i * tb < lay.np_rows

        @pl.when(is_p)
        def _():
            yp_ref[...] = out

        @pl.when(jnp.logical_not(is_p))
        def _():
            ys_ref[...] = out


def _ffn(lay, x, mods, layer, g, wup, wdw, wdn, final_g=None):
    tb = FFN_TB
    d = lay.d
    nf = wdn.shape[0]
    n = tb + 2 * HALO
    lay.check_block(tb)
    assert n % HALO == 0 and d % 128 == 0
    final = final_g is not None
    blk = pl.BlockSpec((tb, d), lambda i: (i, 0))
    if final:
        npb = lay.np_rows // tb
        out_shape = [jax.ShapeDtypeStruct((lay.np_rows, d), F32), jax.ShapeDtypeStruct((lay.ns_rows, d), F32)]
        out_specs = [pl.BlockSpec((tb, d), lambda i: (jnp.minimum(i, npb - 1), 0)),
                     pl.BlockSpec((tb, d), lambda i: (jnp.maximum(i - npb, 0), 0))]
    else:
        out_shape = jax.ShapeDtypeStruct((lay.nt, d), F32)
        out_specs = blk
    return pl.pallas_call(
        functools.partial(_ffn_kernel, lay=lay, tb=tb, nf=nf, final=final),
        out_shape=out_shape,
        grid=(lay.nt // tb,),
        in_specs=_halo_specs(lay, tb) + [
            _mod_spec(lay, layer, tb),
            _const_spec((1, d)),
            _const_spec(wup.shape),
            _const_spec(wdw.shape),
            _const_spec(wdn.shape),
        ] + ([_const_spec((1, d))] if final else []),
        out_specs=out_specs,
        scratch_shapes=[pltpu.VMEM((d // 128, n, 128), F32),
                        pltpu.VMEM((2, 2, n, FFN_FC), F32)],
        compiler_params=_cparams(("arbitrary",)),
        name="conv_ffn",
    )(x, x, x, mods, g, wup, wdw, wdn, *([final_g] if final else []))


def _sconv_kernel(*refs, lay, tb, embed):
    if embed:
        (xp_ref, xpp_ref, xpn_ref, xs_ref, xsp_ref, xsn_ref, ps_ref, psp_ref, psn_ref,
         mod_ref, g_ref, win_ref, wdw_ref, wout_ref, o_ref, xh_ref, p_ref) = refs
    else:
        x_ref, xp_ref, xn_ref, mod_ref, g_ref, win_ref, wdw_ref, wout_ref, o_ref, xh_ref, p_ref = refs
    d = lay.d
    n = tb + 2 * HALO
    i = pl.program_id(0)
    pos, L = _pos_in_seq(lay, i * tb, tb)
    first, last = pos == 0, pos == L - 1
    shift, scale, gate = _mod_parts(mod_ref, d, first=True)
    if embed:
        is_p = i * tb < lay.np_rows

        @pl.when(is_p)
        def _():
            xh_ref[0:HALO, :] = xpp_ref[...]
            xh_ref[HALO:HALO + tb, :] = xp_ref[...]
            xh_ref[HALO + tb:n, :] = xpn_ref[...]

        @pl.when(jnp.logical_not(is_p))
        def _():
            xh_ref[0:HALO, :] = xsp_ref[...] + psp_ref[...]
            xh_ref[HALO:HALO + tb, :] = xs_ref[...] + ps_ref[...]
            xh_ref[HALO + tb:n, :] = xsn_ref[...] + psn_ref[...]
    else:
        xh_ref[0:HALO, :] = xp_ref[...]
        xh_ref[HALO:HALO + tb, :] = x_ref[...]
        xh_ref[HALO + tb:n, :] = xn_ref[...]
    h = _rms_mod(xh_ref[...], g_ref[...], shift, scale).astype(BF16)
    bg = _dot(h, win_ref[:, 0:d])[HALO:HALO + tb]
    p_ref[...] = _dot(h, win_ref[:, d:2 * d]) * _dot(h, win_ref[:, 2 * d:3 * d])
    conv = _dwconv3(p_ref, wdw_ref[...], first, last, tb, _edge_groups(lay, tb))
    y = _dot((bg * conv).astype(BF16), wout_ref[...])
    o_ref[...] = xh_ref[HALO:HALO + tb, :] + gate * y


def _stream_halo_specs(rows, d, tb, blk_of):
    r = tb // HALO
    last8 = rows // HALO - 1
    return [
        pl.BlockSpec((tb, d), lambda i: (blk_of(i), 0)),
        pl.BlockSpec((HALO, d), lambda i: (jnp.clip(blk_of(i) * r - 1, 0, last8), 0)),
        pl.BlockSpec((HALO, d), lambda i: (jnp.clip((blk_of(i) + 1) * r, 0, last8), 0)),
    ]


def _sconv_layer(lay, x, mods, layer, g, w_in, w_dw, w_out, raw=None):
    tb = TB
    d = lay.d
    embed = raw is not None
    if embed:
        npb = lay.np_rows // tb
        psb = lay.ls // tb
        xspecs = (_stream_halo_specs(lay.np_rows, d, tb, lambda i: jnp.minimum(i, npb - 1))
                  + _stream_halo_specs(lay.ns_rows, d, tb, lambda i: jnp.maximum(i - npb, 0))
                  + _stream_halo_specs(lay.ls, d, tb, lambda i: lax.rem(jnp.maximum(i - npb, 0), psb)))
        xargs = (raw[0],) * 3 + (raw[1],) * 3 + (raw[2],) * 3
    else:
        xspecs = _halo_specs(lay, tb)
        xargs = (x, x, x)
    return pl.pallas_call(
        functools.partial(_sconv_kernel, lay=lay, tb=tb, embed=embed),
        out_shape=jax.ShapeDtypeStruct((lay.nt, d), F32),
        grid=(lay.nt // tb,),
        in_specs=xspecs + [
            _mod_spec(lay, layer, tb),
            _const_spec((1, d)),
            _const_spec(w_in.shape),
            _const_spec(w_dw.shape),
            _const_spec(w_out.shape),
        ],
        out_specs=pl.BlockSpec((tb, d), lambda i: (i, 0)),
        scratch_shapes=[pltpu.VMEM((tb + 2 * HALO, d), F32), pltpu.VMEM((tb + 2 * HALO, d), F32)],
        compiler_params=_cparams(("arbitrary",)),
        name="sconv_mixer",
    )(*xargs, mods, g, w_in, w_dw, w_out)


def _pool_kernel(x_ref, xp_ref, xn_ref, mod_ref, g_ref, w_ref, sc_ref, o_ref, h_ref, *, lay, tb):
    d = lay.d
    ng = len(POOL_WINDOWS)
    pg = d // ng
    i = pl.program_id(0)
    pos, L = _pos_in_seq(lay, i * tb, tb)
    shift, scale, gate = _mod_parts(mod_ref, d, first=True)
    xh = jnp.concatenate([xp_ref[...], x_ref[...], xn_ref[...]], axis=0)
    h_ref[...] = _rms_mod(xh, g_ref[...], shift, scale)
    edges = _edge_groups(lay, tb)
    outs = []
    for gi, w in enumerate(POOL_WINDOWS):
        assert w // 2 <= HALO
        cols = slice(gi * pg, (gi + 1) * pg)
        centre = h_ref[HALO:HALO + tb, cols]
        s = centre
        cnt = jnp.ones((tb, 1), F32)
        for k in range(-(w // 2), w // 2):
            if k == 0:
                continue
            outside = (pos + k < 0) if k < 0 else (pos + k > L - 1)
            sh = h_ref[HALO + k:HALO + k + tb, cols]
            s = s + _zero_rows(sh, outside, edges[0] if k < 0 else edges[1])
            cnt = cnt + jnp.where(outside, 0.0, 1.0)
        pooled = (s / cnt - centre).astype(BF16)
        outs.append(_dot(pooled, w_ref[gi]))
    y = jnp.concatenate(outs, axis=-1) * sc_ref[...]
    o_ref[...] = x_ref[...] + gate * y


def _pool_layer(lay, x, mods, layer, g, w, sc):
    tb = TB
    d = lay.d
    return pl.pallas_call(
        functools.partial(_pool_kernel, lay=lay, tb=tb),
        out_shape=jax.ShapeDtypeStruct((lay.nt, d), F32),
        grid=(lay.nt // tb,),
        in_specs=_halo_specs(lay, tb) + [
            _mod_spec(lay, layer, tb),
            _const_spec((1, d)),
            _const_spec(w.shape),
            _const_spec((1, d)),
        ],
        out_specs=pl.BlockSpec((tb, d), lambda i: (i, 0)),
        scratch_shapes=[pltpu.VMEM((tb + 2 * HALO, d), F32)],
        compiler_params=_cparams(("parallel",)),
        name="pool_mixer",
    )(x, x, x, mods, g, w, sc)


def _sgu_kernel(x_ref, mod_ref, g_ref, win_ref, ng_ref, ws_ref, bs_ref, wout_ref, o_ref, s_ref, *, lay, tb):
    d = lay.d
    groups = ws_ref.shape[0]
    gd = d // groups
    shift, scale, gate = _mod_parts(mod_ref, d, first=True)
    x = x_ref[...]
    h = _rms_mod(x, g_ref[...], shift, scale).astype(BF16)
    u = jax.nn.gelu(_dot(h, win_ref[:, 0:d]), approximate=True)
    v = jax.nn.gelu(_dot(h, win_ref[:, d:2 * d]), approximate=True)
    ms = jnp.mean(v * v, axis=-1, keepdims=True)
    vb = (v * lax.rsqrt(ms + RMS_EPS) * ng_ref[...]).astype(BF16)
    for n in range(tb // SGU_CHUNK):
        for gi in range(groups):
            rows = slice(n * SGU_CHUNK, (n + 1) * SGU_CHUNK)
            cols = slice(gi * gd, (gi + 1) * gd)
            s_ref[rows, cols] = _dot(ws_ref[gi], vb[rows, cols]) + bs_ref[:, gi:gi + 1]
    y = _dot((u * s_ref[...]).astype(BF16), wout_ref[...])
    o_ref[...] = x + gate * y


def _sgu_layer(lay, x, mods, layer, g, w_in, norm_g, w_s, b_st, w_out):
    tb = TB
    d = lay.d
    assert tb % SGU_CHUNK == 0 and lay.lp % SGU_CHUNK == 0 and lay.ls % SGU_CHUNK == 0
    return pl.pallas_call(
        functools.partial(_sgu_kernel, lay=lay, tb=tb),
        out_shape=jax.ShapeDtypeStruct((lay.nt, d), F32),
        grid=(lay.nt // tb,),
        in_specs=[
            pl.BlockSpec((tb, d), lambda i: (i, 0)),
            _mod_spec(lay, layer, tb),
            _const_spec((1, d)),
            _const_spec(w_in.shape),
            _const_spec((1, d)),
            _const_spec(w_s.shape),
            _const_spec(b_st.shape),
            _const_spec(w_out.shape),
        ],
        out_specs=pl.BlockSpec((tb, d), lambda i: (i, 0)),
        scratch_shapes=[pltpu.VMEM((tb, d), F32)],
        compiler_params=_cparams(("parallel",)),
        name="sgu_mixer",
    )(x, mods, g, w_in, norm_g, w_s, b_st, w_out)


def _split2(a):
    hi = a.astype(BF16)
    lo = (a - hi.astype(F32)).astype(BF16)
    return hi, lo


def _hgrn_gates(z, lb, one_m_lb):
    ez = jnp.exp(-jnp.abs(z))
    r = 1.0 / (1.0 + ez)
    er = ez * r
    pos = z >= 0.0
    f = lb + one_m_lb * jnp.where(pos, r, er)
    logf = jnp.log(jnp.maximum(f, F32_TINY))
    return logf, one_m_lb * jnp.where(pos, er, r)


def _hgrn_proj_kernel(x_ref, mod_ref, g_ref, win_ref, lb_ref,
                      qf_ref, kf_ref, qb_ref, kb_ref, v_ref, sg_ref, cv_ref, *, lay, tb, layer):
    d = lay.d
    c = HGRN_C
    shift, scale, _ = _mod_parts(mod_ref, d, first=True)
    h = _rms_mod(x_ref[...], g_ref[...], shift, scale).astype(BF16)

    lbp = lb_ref[...]
    e = jnp.exp(lbp - jnp.max(lbp, axis=0, keepdims=True))
    p = e / jnp.sum(e, axis=0, keepdims=True)
    lb = jnp.zeros((1, 2 * d), F32)
    for j in range(1, layer + 1):
        lb = lb + p[j:j + 1]
    one_m_lb = 1.0 - lb

    z = [_dot(h, win_ref[:, (1 + di) * d:(2 + di) * d]) for di in range(2)]
    q = _dot(h, win_ref[:, 0:d])
    gates = [_hgrn_gates(z[di], lb[:, di * d:(di + 1) * d], one_m_lb[:, di * d:(di + 1) * d]) for di in range(2)]

    rt = lax.broadcasted_iota(jnp.int32, (c, 2 * c), 0)
    cs = lax.broadcasted_iota(jnp.int32, (c, 2 * c), 1)
    cs = jnp.where(cs >= c, cs - c, cs)
    tri_f = (cs <= rt).astype(BF16)
    tri_b = (cs >= rt).astype(BF16)
    mid = c // 2

    def cumsum(di, tri):
        logf = gates[di][0]
        return [_dot(tri, jnp.concatenate(_split2(logf[n * c:(n + 1) * c]), axis=0)) for n in range(tb // c)]

    cum = [cumsum(0, tri_f)]
    vv = _dot(h, win_ref[:, 3 * d:4 * d])
    cum.append(cumsum(1, tri_b))
    gg = _dot(h, win_ref[:, 4 * d:5 * d])

    for di, (q_ref, k_ref) in enumerate(((qf_ref, kf_ref), (qb_ref, kb_ref))):
        kk = gates[di][1]
        for n in range(tb // c):
            rows = slice(n * c, (n + 1) * c)
            b = cum[di][n]
            if di == 0:
                btot = b[c - 1:c]
                bref = b[mid - 1:mid]
            else:
                btot = b[0:1]
                bref = b[mid:mid + 1]
            q_ref[rows, :] = (q[rows] * jnp.exp(jnp.minimum(b - bref, EXP_CLAMP))).astype(BF16)
            k_ref[rows, :] = (kk[rows] * jnp.exp(jnp.minimum(bref - b, EXP_CLAMP))).astype(BF16)
            cv_ref[n, 3 * di:3 * di + 1, :] = jnp.exp(bref)
            cv_ref[n, 3 * di + 1:3 * di + 2, :] = jnp.exp(btot - bref)
            cv_ref[n, 3 * di + 2:3 * di + 3, :] = jnp.exp(btot)
    v_ref[...] = vv.astype(BF16)
    sg_ref[...] = (gg * jax.nn.sigmoid(gg)).astype(BF16)
    for n in range(tb // c):
        cv_ref[n, 6:8, :] = jnp.zeros((2, d), F32)


def _hgrn_proj(lay, x, mods, layer, g, w_in, lb2):
    tb = TB
    d = lay.d
    c = HGRN_C
    blk = pl.BlockSpec((tb, d), lambda i: (i, 0))
    act = jax.ShapeDtypeStruct((lay.nt, d), BF16)
    return pl.pallas_call(
        functools.partial(_hgrn_proj_kernel, lay=lay, tb=tb, layer=layer),
        out_shape=[act, act, act, act, act, act, jax.ShapeDtypeStruct((lay.nt // c, 8, d), F32)],
        grid=(lay.nt // tb,),
        in_specs=[blk, _mod_spec(lay, layer, tb), _const_spec((1, d)), _const_spec(w_in.shape),
                  _const_spec(lb2.shape)],
        out_specs=[blk, blk, blk, blk, blk, blk, pl.BlockSpec((tb // c, 8, d), lambda i: (i, 0, 0))],
        compiler_params=_cparams(("parallel",)),
        name="hgrn_proj",
    )(x, mods, g, w_in, lb2)


def _hgrn_scan_kernel(qf_ref, kf_ref, vf_ref, cvf_ref, qb_ref, kb_ref, vb_ref, cvb_ref, s0_ref,
                      of_ref, ob_ref, sfin_ref, st_ref, *, lay, tb, heads):
    d = lay.d
    c = HGRN_C
    dk = d // heads
    i = pl.program_id(0)
    r0 = i * tb
    is_p = r0 < lay.np_rows
    base = jnp.where(is_p, lax.rem(r0, lay.lp), lax.rem(jnp.maximum(r0 - lay.np_rows, 0), lay.ls))
    L = jnp.where(is_p, lay.lp, lay.ls)

    @pl.when(base == 0)
    def _():
        for di in range(2):
            for hd in range(heads):
                s0 = jnp.where(is_p, 0.0, s0_ref[di, hd])
                st_ref[di, hd] = s0.T

    rt = lax.broadcasted_iota(jnp.int32, (c, c), 0)
    cs = lax.broadcasted_iota(jnp.int32, (c, c), 1)
    keep = (cs <= rt, cs >= rt)
    nchunk = tb // c
    dirs = ((qf_ref, kf_ref, vf_ref, cvf_ref, of_ref), (qb_ref, kb_ref, vb_ref, cvb_ref, ob_ref))
    nt_dims = (((1,), (1,)), ((), ()))
    tn_dims = (((0,), (0,)), ((), ()))
    st = [[st_ref[di, hd] for hd in range(heads)] for di in range(2)]
    for n in range(nchunk):
        part = {}
        for di, (q_ref, k_ref, v_ref, cv_ref, _) in enumerate(dirs):
            nn = n if di == 0 else nchunk - 1 - n
            rows = slice(nn * c, (nn + 1) * c)
            for hd in range(heads):
                cols = slice(hd * dk, (hd + 1) * dk)
                qs, ks, vv = q_ref[rows, cols], k_ref[rows, cols], v_ref[rows, cols]
                ku = (ks.astype(F32) * cv_ref[nn, 3 * di + 1:3 * di + 2, cols]).astype(BF16)
                sc = lax.dot_general(qs, ks, nt_dims, preferred_element_type=F32)
                upd = lax.dot_general(vv, ku, tn_dims, preferred_element_type=F32)
                part[di, hd] = (qs, vv, sc, upd)
        for di, (_, _, _, cv_ref, o_ref) in enumerate(dirs):
            nn = n if di == 0 else nchunk - 1 - n
            rows = slice(nn * c, (nn + 1) * c)
            for hd in range(heads):
                cols = slice(hd * dk, (hd + 1) * dk)
                qs, vv, sc, upd = part[di, hd]
                scb = jnp.where(keep[di], sc, 0.0).astype(BF16)
                qi = (qs.astype(F32) * cv_ref[nn, 3 * di:3 * di + 1, cols]).astype(BF16)
                o_ref[rows, cols] = (_dot(scb, vv) + lax.dot_general(
                    qi, st[di][hd].astype(BF16), nt_dims, preferred_element_type=F32)).astype(BF16)
                st[di][hd] = st[di][hd] * cv_ref[nn, 3 * di + 2:3 * di + 3, cols] + upd
    for di in range(2):
        for hd in range(heads):
            st_ref[di, hd] = st[di][hd]

    @pl.when(base + tb == L)
    def _():
        for di in range(2):
            for hd in range(heads):
                sfin_ref[di, hd] = st_ref[di, hd].T


def _hgrn_scan(lay, qf, kf, qb, kb, v, cv, s0, heads):
    tb = HGRN_TB
    d = lay.d
    c = HGRN_C
    dk = d // heads
    lay.check_block(tb)
    assert tb <= lay.lp and tb <= lay.ls
    npb = lay.np_rows // tb
    bps_p, bps_s = lay.lp // tb, lay.ls // tb
    nseq = lay.batch_p + lay.batch_s

    def mirror(i):
        ip = (i // bps_p) * bps_p + (bps_p - 1 - lax.rem(i, bps_p))
        j = jnp.maximum(i - npb, 0)
        isx = npb + (j // bps_s) * bps_s + (bps_s - 1 - lax.rem(j, bps_s))
        return jnp.where(i < npb, ip, isx)

    def seq_of(i):
        return jnp.where(i < npb, i // bps_p, lay.batch_p + jnp.maximum(i - npb, 0) // bps_s)

    fwd = pl.BlockSpec((tb, d), lambda i: (i, 0))
    bwd = pl.BlockSpec((tb, d), lambda i: (mirror(i), 0))
    cvf = pl.BlockSpec((tb // c, 8, d), lambda i: (i, 0, 0))
    cvb = pl.BlockSpec((tb // c, 8, d), lambda i: (mirror(i), 0, 0))
    st_blk = (None, 2, heads, dk, dk)
    s0_spec = pl.BlockSpec(st_blk, lambda i: (jnp.clip(seq_of(i) - lay.batch_p, 0, lay.batch_s - 1), 0, 0, 0, 0))
    sfin_spec = pl.BlockSpec(st_blk, lambda i: (seq_of(i), 0, 0, 0, 0))
    return pl.pallas_call(
        functools.partial(_hgrn_scan_kernel, lay=lay, tb=tb, heads=heads),
        out_shape=[jax.ShapeDtypeStruct((lay.nt, d), BF16), jax.ShapeDtypeStruct((lay.nt, d), BF16),
                   jax.ShapeDtypeStruct((nseq, 2, heads, dk, dk), F32)],
        grid=(lay.nt // tb,),
        in_specs=[fwd, fwd, fwd, cvf, bwd, bwd, bwd, cvb, s0_spec],
        out_specs=[fwd, bwd, sfin_spec],
        scratch_shapes=[pltpu.VMEM((2, heads, dk, dk), F32)],
        compiler_params=_cparams(("arbitrary",)),
        name="hgrn_scan",
    )(qf, kf, v, cv, qb, kb, v, cv, s0)


def _hgrn_out_kernel(x_ref, of_ref, ob_ref, sg_ref, mod_ref, ng_ref, wout_ref, o_ref, *, lay, heads):
    d = lay.d
    dk = d // heads
    _, _, gate = _mod_parts(mod_ref, d, first=True)
    o = of_ref[...].astype(F32) + ob_ref[...].astype(F32)
    parts = []
    for hd in range(heads):
        oh = o[:, hd * dk:(hd + 1) * dk]
        ms = jnp.mean(oh * oh, axis=-1, keepdims=True)
        parts.append(oh * lax.rsqrt(ms + RMS_EPS))
    on = jnp.concatenate(parts, axis=-1) * ng_ref[...]
    y = _dot((on * sg_ref[...].astype(F32)).astype(BF16), wout_ref[...])
    o_ref[...] = x_ref[...] + gate * y


def _hgrn_out(lay, x, o_f, o_b, sg, mods, layer, norm_g, w_out, heads):
    tb = TB
    d = lay.d
    blk = pl.BlockSpec((tb, d), lambda i: (i, 0))
    return pl.pallas_call(
        functools.partial(_hgrn_out_kernel, lay=lay, heads=heads),
        out_shape=jax.ShapeDtypeStruct((lay.nt, d), F32),
        grid=(lay.nt // tb,),
        in_specs=[blk, blk, blk, blk, _mod_spec(lay, layer, tb), _const_spec((1, d)), _const_spec(w_out.shape)],
        out_specs=blk,
        compiler_params=_cparams(("parallel",)),
        name="hgrn_out",
    )(x, o_f, o_b, sg, mods, norm_g, w_out)


def kernel(x_prompt, x_sample, state_rec, c, c_ctx, ada_w, ada_b, norm_g, final_g, conv_w_in, conv_w_dw,
           conv_w_out, pool_w, pool_scale, sgu_w_in, sgu_norm_g, sgu_w_s, sgu_b_s, sgu_w_out, hgrn_w_in,
           hgrn_lb, hgrn_norm_g, hgrn_w_out, ffn_w_up, ffn_w_dw, ffn_w_down):
    bp, lp, d = x_prompt.shape
    bs, ls, _ = x_sample.shape
    depth = ada_w.shape[0]
    heads = state_rec.shape[3]
    d_ff = ffn_w_down.shape[1]
    lay = _Layout(bp, lp, bs, ls, d)
    lay.check_block(TB)
    assert c.shape[0] == bs and 1 + bs <= 8 and d_ff % FFN_FC == 0
    nf = d_ff // FFN_FC

    cvec = jnp.concatenate([c_ctx[None], c, jnp.zeros((8 - 1 - bs, d), F32)], axis=0)
    mods = _ada_params(cvec, ada_w, ada_b).reshape(depth * 8, 1, 6 * d)

    assert N_MIXERS >= 1 and depth >= 1
    raw = (x_prompt.reshape(bp * lp, d), x_sample.reshape(bs * ls, d), _pos_table(ls, d))
    x = None

    new_states = []
    for i in range(depth):
        kind, j = i % N_MIXERS, i // N_MIXERS
        g1 = norm_g[i, 0].reshape(1, d)
        g2 = norm_g[i, 1].reshape(1, d)
        if kind == 0:
            x = _sconv_layer(lay, x, mods, i, g1, conv_w_in[j].astype(BF16), conv_w_dw[j],
                             conv_w_out[j].astype(BF16), raw=raw if i == 0 else None)
        elif kind == 1:
            x = _pool_layer(lay, x, mods, i, g1, pool_w[j].astype(BF16), pool_scale[j].reshape(1, d))
        elif kind == 2:
            x = _sgu_layer(lay, x, mods, i, g1, sgu_w_in[j].astype(BF16), sgu_norm_g[j].reshape(1, d),
                           sgu_w_s[j].astype(BF16), sgu_b_s[j].T, sgu_w_out[j].astype(BF16))
        else:
            qf, kf, qb, kb, v, sg, cv = _hgrn_proj(lay, x, mods, i, g1, hgrn_w_in[j].astype(BF16),
                                                   hgrn_lb.reshape(depth, 2 * d))
            o_f, o_b, sfin = _hgrn_scan(lay, qf, kf, qb, kb, v, cv, state_rec[:, j], heads)
            new_states.append(sfin[:bp])
            x = _hgrn_out(lay, x, o_f, o_b, sg, mods, i, hgrn_norm_g[j].reshape(1, d),
                          hgrn_w_out[j].astype(BF16), heads)
        wup = ffn_w_up[i].reshape(d, 2, nf, FFN_FC).transpose(1, 2, 0, 3).astype(BF16)
        wdw = ffn_w_dw[i].reshape(3, 2, nf, FFN_FC).transpose(1, 2, 0, 3)
        wdn = ffn_w_down[i].reshape(nf, FFN_FC, d).astype(BF16)
        if i + 1 < depth:
            x = _ffn(lay, x, mods, i, g2, wup, wdw, wdn)
        else:
            y_prompt, y_sample = _ffn(lay, x, mods, i, g2, wup, wdw, wdn, final_g.reshape(1, d))
    y_prompt = y_prompt.reshape(bp, lp, d)
    y_sample = y_sample.reshape(bs, ls, d)
    new_state_rec = jnp.stack(new_states, axis=1)
    return (y_prompt, y_sample, new_state_rec)
```

```python
import functools
import math

import jax
import jax.numpy as jnp
from jax import lax
from jax.experimental import pallas as pl
from jax.experimental.pallas import tpu as pltpu

GRID_W = 64
N_MIXERS = 4
RMS_EPS = 1e-6
POOL_WINDOWS = (2, 4, 8, 16)
SGU_CHUNK = 128
POS_BASE = 10000.0

HALO = 8
TB = 512
FFN_TB = 512
FFN_FC = 256
HGRN_C = 64
HGRN_TB = 256
EXP_CLAMP = 80.0
F32_TINY = 1e-37
VMEM_LIMIT = 60000 * 1024

BF16 = jnp.bfloat16
F32 = jnp.float32


def _cparams(sem):
    return pltpu.CompilerParams(dimension_semantics=sem, vmem_limit_bytes=VMEM_LIMIT)


def _dot(a, b):
    return jnp.dot(a, b, preferred_element_type=F32)


def _const_spec(shape):
    nd = len(shape)
    return pl.BlockSpec(shape, lambda i: (0,) * nd)


def _layer_spec(stacked, idx):
    rest = stacked.shape[1:]
    return pl.BlockSpec((None,) + rest, lambda i: (idx,) + (0,) * len(rest))


class _Layout:
    def __init__(self, batch_p, seq_p, batch_s, seq_s, d):
        self.lp, self.ls, self.d = seq_p, seq_s, d
        self.np_rows = batch_p * seq_p
        self.ns_rows = batch_s * seq_s
        self.nt = self.np_rows + self.ns_rows
        self.batch_p, self.batch_s = batch_p, batch_s

    def check_block(self, tb):
        assert self.np_rows % tb == 0 and self.ns_rows % tb == 0
        for L in (self.lp, self.ls):
            assert L % tb == 0 or tb % L == 0

    def cond_row(self, r0):
        return jnp.where(r0 < self.np_rows, 0, 1 + (jnp.maximum(r0 - self.np_rows, 0)) // self.ls)


def _pos_in_seq(lay, r0, tb):
    k = lax.broadcasted_iota(jnp.int32, (tb, 1), 0)

    def pos_for(start, L):
        base = lax.rem(jnp.maximum(r0 - start, 0), L)
        p = base + k
        for j in range(1, (tb + L - 1) // L + 1):
            p = p - jnp.where(base + k >= j * L, L, 0)
        return p

    is_p = r0 < lay.np_rows
    pos = jnp.where(is_p, pos_for(0, lay.lp), pos_for(lay.np_rows, lay.ls))
    L = jnp.where(is_p, lay.lp, lay.ls)
    return pos, L


def _rms_mod(x, g, shift, scale):
    ms = jnp.mean(x * x, axis=-1, keepdims=True)
    return x * lax.rsqrt(ms + RMS_EPS) * (g * (1.0 + scale)) + shift


def _mod_parts(mod_ref, d, first):
    o = 0 if first else 3 * d
    return (mod_ref[:, o:o + d], mod_ref[:, o + d:o + 2 * d], mod_ref[:, o + 2 * d:o + 3 * d])


def _halo_specs(lay, tb):
    nb8 = lay.nt // HALO
    r = tb // HALO
    return [
        pl.BlockSpec((tb, lay.d), lambda i: (i, 0)),
        pl.BlockSpec((HALO, lay.d), lambda i: (jnp.maximum(i * r - 1, 0), 0)),
        pl.BlockSpec((HALO, lay.d), lambda i: (jnp.minimum((i + 1) * r, nb8 - 1), 0)),
    ]


def _mod_spec(lay, layer, tb):
    return pl.BlockSpec((None, 1, 6 * lay.d), lambda i: (layer * 8 + lay.cond_row(i * tb), 0, 0))


def _edge_groups(lay, tb):
    step = math.gcd(math.gcd(lay.lp, lay.ls), tb)
    assert step % HALO == 0
    firsts = list(range(0, tb, step))
    lasts = [(k - HALO) % tb for k in firsts]
    return firsts, lasts


def _zero_rows(a, mask, groups):
    pieces, cur = [], 0
    for r in sorted(groups):
        if r > cur:
            pieces.append(a[cur:r])
        pieces.append(jnp.where(mask[r:r + HALO], 0.0, a[r:r + HALO]))
        cur = r + HALO
    if cur < a.shape[0]:
        pieces.append(a[cur:])
    return jnp.concatenate(pieces, axis=0)


def _dwconv3(a_ref, w, first, last, tb, edges):
    prev = a_ref[HALO - 1:HALO - 1 + tb, :]
    mid = a_ref[HALO:HALO + tb, :]
    nxt = a_ref[HALO + 1:HALO + 1 + tb, :]
    prev = _zero_rows(prev, first, edges[0])
    nxt = _zero_rows(nxt, last, edges[1])
    return prev * w[0:1] + mid * w[1:2] + nxt * w[2:3]


def _pos_kernel(o_ref, *, tb, d):
    i = pl.program_id(0)
    q = d // 4
    nr = tb // GRID_W
    j = lax.broadcasted_iota(jnp.int32, (1, q), 1).astype(F32)
    freq = jnp.exp(-math.log(POS_BASE) * j / q)
    rr = (i * nr + lax.broadcasted_iota(jnp.int32, (nr, 1), 0)).astype(F32)
    cc = lax.broadcasted_iota(jnp.int32, (GRID_W, 1), 0).astype(F32)
    ar = rr * freq
    ac = cc * freq
    row_part = jnp.concatenate([jnp.sin(ar), jnp.cos(ar)], axis=1)
    col_part = jnp.concatenate([jnp.sin(ac), jnp.cos(ac)], axis=1)
    for r in range(nr):
        rows = slice(r * GRID_W, (r + 1) * GRID_W)
        o_ref[rows, 0:2 * q] = jnp.broadcast_to(row_part[r:r + 1], (GRID_W, 2 * q))
        o_ref[rows, 2 * q:4 * q] = col_part


def _pos_table(n_tokens, d):
    tb = 512
    assert tb % GRID_W == 0 and n_tokens % tb == 0
    return pl.pallas_call(
        functools.partial(_pos_kernel, tb=tb, d=d),
        out_shape=jax.ShapeDtypeStruct((n_tokens, d), F32),
        grid=(n_tokens // tb,),
        out_specs=pl.BlockSpec((tb, d), lambda i: (i, 0)),
        compiler_params=_cparams(("parallel",)),
        name="pos_table",
    )()


def _ada_kernel(c_ref, w_ref, b_ref, o_ref):
    c = c_ref[...]
    s = (c * jax.nn.sigmoid(c)).astype(BF16)
    o_ref[...] = _dot(s, w_ref[...].astype(BF16)) + b_ref[...]


def _ada_params(cvec, ada_w, ada_b):
    depth, d, n = ada_w.shape
    tn = 1536
    assert n % tn == 0
    return pl.pallas_call(
        _ada_kernel,
        out_shape=jax.ShapeDtypeStruct((depth, 8, n), F32),
        grid=(depth, n // tn),
        in_specs=[
            pl.BlockSpec((8, d), lambda l, j: (0, 0)),
            pl.BlockSpec((None, d, tn), lambda l, j: (l, 0, j)),
            pl.BlockSpec((None, 1, tn), lambda l, j: (l, 0, j)),
        ],
        out_specs=pl.BlockSpec((None, 8, tn), lambda l, j: (l, 0, j)),
        compiler_params=_cparams(("parallel", "parallel")),
        name="ada_params",
    )(cvec, ada_w, ada_b.reshape(depth, 1, n))


def _edge_points(lay, r0, tb):
    is_p = r0 < lay.np_rows
    base = jnp.where(is_p, lax.rem(r0, lay.lp), lax.rem(jnp.maximum(r0 - lay.np_rows, 0), lay.ls))
    L = jnp.where(is_p, lay.lp, lay.ls)
    step = math.gcd(math.gcd(lay.lp, lay.ls), tb)
    firsts = [(k, lax.rem(base + k, L) == 0) for k in range(0, tb, step)]
    lasts = [((k - 1) % tb, lax.rem(base + ((k - 1) % tb) + 1, L) == 0) for k in range(0, tb, step)]
    return firsts, lasts


def _zero_points(a, points, s_rows):
    sub = lax.broadcasted_iota(jnp.int32, (HALO, 1), 0)
    pieces, cur = [], 0
    for k, flag in sorted(points, key=lambda p: (p[0] + HALO) % s_rows):
        s, j = divmod(k + HALO, s_rows)
        r = HALO * j
        assert r >= cur
        if r > cur:
            pieces.append(a[cur:r])
        pieces.append(jnp.where(jnp.logical_and(sub == s, flag), 0.0, a[r:r + HALO]))
        cur = r + HALO
    if cur < a.shape[0]:
        pieces.append(a[cur:])
    return jnp.concatenate(pieces, axis=0)


def _dwconv3_strided(u_ref, w, firsts, lasts):
    n = u_ref.shape[0]
    s_rows = n // HALO
    mid = u_ref[...]
    prev = jnp.concatenate([pltpu.roll(u_ref[n - HALO:n, :], 1, axis=0), u_ref[0:n - HALO, :]], axis=0)
    nxt = jnp.concatenate([u_ref[HALO:n, :], pltpu.roll(u_ref[0:HALO, :], HALO - 1, axis=0)], axis=0)
    prev = _zero_points(prev, firsts, s_rows)
    nxt = _zero_points(nxt, lasts, s_rows)
    return prev * w[0:1] + mid * w[1:2] + nxt * w[2:3]


def _ffn_kernel(*refs, lay, tb, nf, final):
    if final:
        (x_ref, xp_ref, xn_ref, mod_ref, g_ref, wup_ref, wdw_ref, wdn_ref, fg_ref,
         yp_ref, ys_ref, slab_ref, u_ref) = refs
    else:
        x_ref, xp_ref, xn_ref, mod_ref, g_ref, wup_ref, wdw_ref, wdn_ref, o_ref, slab_ref, u_ref = refs
    d = lay.d
    fc = FFN_FC
    n = tb + 2 * HALO
    s_rows = n // HALO
    nl = d // 128
    i = pl.program_id(0)
    firsts, lasts = _edge_points(lay, i * tb, tb)
    shift, scale, gate = _mod_parts(mod_ref, d, first=False)

    for l in range(nl):
        cols = slice(128 * l, 128 * (l + 1))
        slab_ref[l, 0:HALO, :] = xp_ref[:, cols]
        slab_ref[l, HALO:HALO + tb, :] = x_ref[:, cols]
        slab_ref[l, HALO + tb:n, :] = xn_ref[:, cols]
    xs = jnp.concatenate(
        [jnp.concatenate([slab_ref[l, pl.ds(j, HALO, stride=s_rows), :] for l in range(nl)], axis=1)
         for j in range(s_rows)], axis=0)
    h = _rms_mod(xs, g_ref[...], shift, scale).astype(BF16)

    def up(c):
        for ab in range(2):
            u_ref[c % 2, ab] = _dot(h, wup_ref[:, ab * nf * fc + c * fc:ab * nf * fc + (c + 1) * fc])

    up(0)
    acc = None
    for c in range(nf):
        if c + 1 < nf:
            up(c + 1)
        ca = _dwconv3_strided(u_ref.at[c % 2, 0], wdw_ref[:, c * fc:(c + 1) * fc], firsts, lasts)
        cb = _dwconv3_strided(u_ref.at[c % 2, 1], wdw_ref[:, (nf + c) * fc:(nf + c + 1) * fc], firsts, lasts)
        act = (ca * jax.nn.sigmoid(ca) * cb).astype(BF16)
        part = _dot(act, wdn_ref[c * fc:(c + 1) * fc, :])
        acc = part if acc is None else acc + part

    for l in range(nl):
        for j in range(s_rows):
            slab_ref[l, pl.ds(j, HALO, stride=s_rows), :] = acc[HALO * j:HALO * (j + 1), 128 * l:128 * (l + 1)]
    y = jnp.concatenate([slab_ref[l, HALO:HALO + tb, :] for l in range(nl)], axis=1)
    out = x_ref[...] + gate * y
    if not final:
        o_ref[...] = out
    else:
        ms = jnp.mean(out * out, axis=-1, keepdims=True)
        out = out * lax.rsqrt(ms + RMS_EPS) * fg_ref[...]
        is_p = i * tb < lay.np_rows

        @pl.when(is_p)
        def _():
            yp_ref[...] = out

        @pl.when(jnp.logical_not(is_p))
        def _():
            ys_ref[...] = out


def _ffn(lay, x, mods, layer, g, wup, wdw, wdn, final_g=None):
    tb = FFN_TB
    d = lay.d
    assert wdn.shape[1] % FFN_FC == 0
    nf = wdn.shape[1] // FFN_FC
    n = tb + 2 * HALO
    lay.check_block(tb)
    assert n % HALO == 0 and d % 128 == 0
    final = final_g is not None
    blk = pl.BlockSpec((tb, d), lambda i: (i, 0))
    if final:
        npb = lay.np_rows // tb
        out_shape = [jax.ShapeDtypeStruct((lay.np_rows, d), F32), jax.ShapeDtypeStruct((lay.ns_rows, d), F32)]
        out_specs = [pl.BlockSpec((tb, d), lambda i: (jnp.minimum(i, npb - 1), 0)),
                     pl.BlockSpec((tb, d), lambda i: (jnp.maximum(i - npb, 0), 0))]
    else:
        out_shape = jax.ShapeDtypeStruct((lay.nt, d), F32)
        out_specs = blk
    return pl.pallas_call(
        functools.partial(_ffn_kernel, lay=lay, tb=tb, nf=nf, final=final),
        out_shape=out_shape,
        grid=(lay.nt // tb,),
        in_specs=_halo_specs(lay, tb) + [
            _mod_spec(lay, layer, tb),
            _const_spec((1, d)),
            _layer_spec(wup, layer),
            _layer_spec(wdw, layer),
            _layer_spec(wdn, layer),
        ] + ([_const_spec((1, d))] if final else []),
        out_specs=out_specs,
        scratch_shapes=[pltpu.VMEM((d // 128, n, 128), F32),
                        pltpu.VMEM((2, 2, n, FFN_FC), F32)],
        compiler_params=_cparams(("arbitrary",)),
        name="conv_ffn",
    )(x, x, x, mods, g, wup, wdw, wdn, *([final_g] if final else []))


def _sconv_kernel(*refs, lay, tb, embed):
    if embed:
        (xp_ref, xpp_ref, xpn_ref, xs_ref, xsp_ref, xsn_ref, ps_ref, psp_ref, psn_ref,
         mod_ref, g_ref, win_ref, wdw_ref, wout_ref, o_ref, xh_ref, p_ref) = refs
    else:
        x_ref, xp_ref, xn_ref, mod_ref, g_ref, win_ref, wdw_ref, wout_ref, o_ref, xh_ref, p_ref = refs
    d = lay.d
    n = tb + 2 * HALO
    i = pl.program_id(0)
    pos, L = _pos_in_seq(lay, i * tb, tb)
    first, last = pos == 0, pos == L - 1
    shift, scale, gate = _mod_parts(mod_ref, d, first=True)
    if embed:
        is_p = i * tb < lay.np_rows

        @pl.when(is_p)
        def _():
            xh_ref[0:HALO, :] = xpp_ref[...]
            xh_ref[HALO:HALO + tb, :] = xp_ref[...]
            xh_ref[HALO + tb:n, :] = xpn_ref[...]

        @pl.when(jnp.logical_not(is_p))
        def _():
            xh_ref[0:HALO, :] = xsp_ref[...] + psp_ref[...]
            xh_ref[HALO:HALO + tb, :] = xs_ref[...] + ps_ref[...]
            xh_ref[HALO + tb:n, :] = xsn_ref[...] + psn_ref[...]
    else:
        xh_ref[0:HALO, :] = xp_ref[...]
        xh_ref[HALO:HALO + tb, :] = x_ref[...]
        xh_ref[HALO + tb:n, :] = xn_ref[...]
    h = _rms_mod(xh_ref[...], g_ref[...], shift, scale).astype(BF16)
    bg = _dot(h, win_ref[:, 0:d])[HALO:HALO + tb]
    p_ref[...] = _dot(h, win_ref[:, d:2 * d]) * _dot(h, win_ref[:, 2 * d:3 * d])
    conv = _dwconv3(p_ref, wdw_ref[...], first, last, tb, _edge_groups(lay, tb))
    y = _dot((bg * conv).astype(BF16), wout_ref[...])
    o_ref[...] = xh_ref[HALO:HALO + tb, :] + gate * y


def _stream_halo_specs(rows, d, tb, blk_of):
    r = tb // HALO
    last8 = rows // HALO - 1
    return [
        pl.BlockSpec((tb, d), lambda i: (blk_of(i), 0)),
        pl.BlockSpec((HALO, d), lambda i: (jnp.clip(blk_of(i) * r - 1, 0, last8), 0)),
        pl.BlockSpec((HALO, d), lambda i: (jnp.clip((blk_of(i) + 1) * r, 0, last8), 0)),
    ]


def _sconv_layer(lay, x, mods, layer, g, j, w_in, w_dw, w_out, raw=None):
    tb = TB
    d = lay.d
    embed = raw is not None
    if embed:
        npb = lay.np_rows // tb
        psb = lay.ls // tb
        xspecs = (_stream_halo_specs(lay.np_rows, d, tb, lambda i: jnp.minimum(i, npb - 1))
                  + _stream_halo_specs(lay.ns_rows, d, tb, lambda i: jnp.maximum(i - npb, 0))
                  + _stream_halo_specs(lay.ls, d, tb, lambda i: lax.rem(jnp.maximum(i - npb, 0), psb)))
        xargs = (raw[0],) * 3 + (raw[1],) * 3 + (raw[2],) * 3
    else:
        xspecs = _halo_specs(lay, tb)
        xargs = (x, x, x)
    return pl.pallas_call(
        functools.partial(_sconv_kernel, lay=lay, tb=tb, embed=embed),
        out_shape=jax.ShapeDtypeStruct((lay.nt, d), F32),
        grid=(lay.nt // tb,),
        in_specs=xspecs + [
            _mod_spec(lay, layer, tb),
            _const_spec((1, d)),
            _layer_spec(w_in, j),
            _layer_spec(w_dw, j),
            _layer_spec(w_out, j),
        ],
        out_specs=pl.BlockSpec((tb, d), lambda i: (i, 0)),
        scratch_shapes=[pltpu.VMEM((tb + 2 * HALO, d), F32), pltpu.VMEM((tb + 2 * HALO, d), F32)],
        compiler_params=_cparams(("arbitrary",)),
        name="sconv_mixer",
    )(*xargs, mods, g, w_in, w_dw, w_out)


def _pool_kernel(x_ref, xp_ref, xn_ref, mod_ref, g_ref, w_ref, sc_ref, o_ref, h_ref, *, lay, tb):
    d = lay.d
    ng = len(POOL_WINDOWS)
    pg = d // ng
    i = pl.program_id(0)
    pos, L = _pos_in_seq(lay, i * tb, tb)
    shift, scale, gate = _mod_parts(mod_ref, d, first=True)
    xh = jnp.concatenate([xp_ref[...], x_ref[...], xn_ref[...]], axis=0)
    h_ref[...] = _rms_mod(xh, g_ref[...], shift, scale)
    edges = _edge_groups(lay, tb)
    outs = []
    for gi, w in enumerate(POOL_WINDOWS):
        assert w // 2 <= HALO
        cols = slice(gi * pg, (gi + 1) * pg)
        centre = h_ref[HALO:HALO + tb, cols]
        s = centre
        cnt = jnp.ones((tb, 1), F32)
        for k in range(-(w // 2), w // 2):
            if k == 0:
                continue
            outside = (pos + k < 0) if k < 0 else (pos + k > L - 1)
            sh = h_ref[HALO + k:HALO + k + tb, cols]
            s = s + _zero_rows(sh, outside, edges[0] if k < 0 else edges[1])
            cnt = cnt + jnp.where(outside, 0.0, 1.0)
        pooled = (s / cnt - centre).astype(BF16)
        outs.append(_dot(pooled, w_ref[gi]))
    y = jnp.concatenate(outs, axis=-1) * sc_ref[...]
    o_ref[...] = x_ref[...] + gate * y


def _pool_layer(lay, x, mods, layer, g, j, w, sc):
    tb = TB
    d = lay.d
    return pl.pallas_call(
        functools.partial(_pool_kernel, lay=lay, tb=tb),
        out_shape=jax.ShapeDtypeStruct((lay.nt, d), F32),
        grid=(lay.nt // tb,),
        in_specs=_halo_specs(lay, tb) + [
            _mod_spec(lay, layer, tb),
            _const_spec((1, d)),
            _layer_spec(w, j),
            _const_spec((1, d)),
        ],
        out_specs=pl.BlockSpec((tb, d), lambda i: (i, 0)),
        scratch_shapes=[pltpu.VMEM((tb + 2 * HALO, d), F32)],
        compiler_params=_cparams(("parallel",)),
        name="pool_mixer",
    )(x, x, x, mods, g, w, sc)


def _sgu_kernel(x_ref, mod_ref, g_ref, win_ref, ng_ref, ws_ref, bs_ref, wout_ref, o_ref, s_ref, *, lay, tb):
    d = lay.d
    groups = ws_ref.shape[0]
    gd = d // groups
    shift, scale, gate = _mod_parts(mod_ref, d, first=True)
    x = x_ref[...]
    h = _rms_mod(x, g_ref[...], shift, scale).astype(BF16)
    u = jax.nn.gelu(_dot(h, win_ref[:, 0:d]), approximate=True)
    v = jax.nn.gelu(_dot(h, win_ref[:, d:2 * d]), approximate=True)
    ms = jnp.mean(v * v, axis=-1, keepdims=True)
    vb = (v * lax.rsqrt(ms + RMS_EPS) * ng_ref[...]).astype(BF16)
    for n in range(tb // SGU_CHUNK):
        for gi in range(groups):
            rows = slice(n * SGU_CHUNK, (n + 1) * SGU_CHUNK)
            cols = slice(gi * gd, (gi + 1) * gd)
            s_ref[rows, cols] = _dot(ws_ref[gi], vb[rows, cols]) + bs_ref[:, gi:gi + 1]
    y = _dot((u * s_ref[...]).astype(BF16), wout_ref[...])
    o_ref[...] = x + gate * y


def _sgu_layer(lay, x, mods, layer, g, j, w_in, norm_g, w_s, b_st, w_out):
    tb = TB
    d = lay.d
    assert tb % SGU_CHUNK == 0 and lay.lp % SGU_CHUNK == 0 and lay.ls % SGU_CHUNK == 0
    return pl.pallas_call(
        functools.partial(_sgu_kernel, lay=lay, tb=tb),
        out_shape=jax.ShapeDtypeStruct((lay.nt, d), F32),
        grid=(lay.nt // tb,),
        in_specs=[
            pl.BlockSpec((tb, d), lambda i: (i, 0)),
            _mod_spec(lay, layer, tb),
            _const_spec((1, d)),
            _layer_spec(w_in, j),
            _const_spec((1, d)),
            _layer_spec(w_s, j),
            _const_spec(b_st.shape),
            _layer_spec(w_out, j),
        ],
        out_specs=pl.BlockSpec((tb, d), lambda i: (i, 0)),
        scratch_shapes=[pltpu.VMEM((tb, d), F32)],
        compiler_params=_cparams(("parallel",)),
        name="sgu_mixer",
    )(x, mods, g, w_in, norm_g, w_s, b_st, w_out)


def _split2(a):
    hi = a.astype(BF16)
    lo = (a - hi.astype(F32)).astype(BF16)
    return hi, lo


def _hgrn_gates(z, lb, one_m_lb):
    ez = jnp.exp(-jnp.abs(z))
    r = 1.0 / (1.0 + ez)
    er = ez * r
    pos = z >= 0.0
    f = lb + one_m_lb * jnp.where(pos, r, er)
    logf = jnp.log(jnp.maximum(f, F32_TINY))
    return logf, one_m_lb * jnp.where(pos, er, r)


def _hgrn_proj_kernel(x_ref, mod_ref, g_ref, win_ref, lb_ref,
                      qf_ref, kf_ref, qb_ref, kb_ref, v_ref, sg_ref, cv_ref, *, lay, tb, layer):
    d = lay.d
    c = HGRN_C
    shift, scale, _ = _mod_parts(mod_ref, d, first=True)
    h = _rms_mod(x_ref[...], g_ref[...], shift, scale).astype(BF16)

    lbp = lb_ref[...]
    e = jnp.exp(lbp - jnp.max(lbp, axis=0, keepdims=True))
    p = e / jnp.sum(e, axis=0, keepdims=True)
    lb = jnp.zeros((1, 2 * d), F32)
    for j in range(1, layer + 1):
        lb = lb + p[j:j + 1]
    one_m_lb = 1.0 - lb

    z = [_dot(h, win_ref[:, (1 + di) * d:(2 + di) * d]) for di in range(2)]
    q = _dot(h, win_ref[:, 0:d])
    gates = [_hgrn_gates(z[di], lb[:, di * d:(di + 1) * d], one_m_lb[:, di * d:(di + 1) * d]) for di in range(2)]

    rt = lax.broadcasted_iota(jnp.int32, (c, 2 * c), 0)
    cs = lax.broadcasted_iota(jnp.int32, (c, 2 * c), 1)
    cs = jnp.where(cs >= c, cs - c, cs)
    tri_f = (cs <= rt).astype(BF16)
    tri_b = (cs >= rt).astype(BF16)
    mid = c // 2

    def cumsum(di, tri):
        logf = gates[di][0]
        return [_dot(tri, jnp.concatenate(_split2(logf[n * c:(n + 1) * c]), axis=0)) for n in range(tb // c)]

    cum = [cumsum(0, tri_f)]
    vv = _dot(h, win_ref[:, 3 * d:4 * d])
    cum.append(cumsum(1, tri_b))
    gg = _dot(h, win_ref[:, 4 * d:5 * d])

    for di, (q_ref, k_ref) in enumerate(((qf_ref, kf_ref), (qb_ref, kb_ref))):
        kk = gates[di][1]
        for n in range(tb // c):
            rows = slice(n * c, (n + 1) * c)
            b = cum[di][n]
            if di == 0:
                btot = b[c - 1:c]
                bref = b[mid - 1:mid]
            else:
                btot = b[0:1]
                bref = b[mid:mid + 1]
            q_ref[rows, :] = (q[rows] * jnp.exp(jnp.minimum(b - bref, EXP_CLAMP))).astype(BF16)
            k_ref[rows, :] = (kk[rows] * jnp.exp(jnp.minimum(bref - b, EXP_CLAMP))).astype(BF16)
            cv_ref[n, 3 * di:3 * di + 1, :] = jnp.exp(bref)
            cv_ref[n, 3 * di + 1:3 * di + 2, :] = jnp.exp(btot - bref)
            cv_ref[n, 3 * di + 2:3 * di + 3, :] = jnp.exp(btot)
    v_ref[...] = vv.astype(BF16)
    sg_ref[...] = (gg * jax.nn.sigmoid(gg)).astype(BF16)
    for n in range(tb // c):
        cv_ref[n, 6:8, :] = jnp.zeros((2, d), F32)


def _hgrn_proj(lay, x, mods, layer, g, j, w_in, lb2):
    tb = TB
    d = lay.d
    c = HGRN_C
    blk = pl.BlockSpec((tb, d), lambda i: (i, 0))
    act = jax.ShapeDtypeStruct((lay.nt, d), BF16)
    return pl.pallas_call(
        functools.partial(_hgrn_proj_kernel, lay=lay, tb=tb, layer=layer),
        out_shape=[act, act, act, act, act, act, jax.ShapeDtypeStruct((lay.nt // c, 8, d), F32)],
        grid=(lay.nt // tb,),
        in_specs=[blk, _mod_spec(lay, layer, tb), _const_spec((1, d)), _layer_spec(w_in, j),
                  _const_spec(lb2.shape)],
        out_specs=[blk, blk, blk, blk, blk, blk, pl.BlockSpec((tb // c, 8, d), lambda i: (i, 0, 0))],
        compiler_params=_cparams(("parallel",)),
        name="hgrn_proj",
    )(x, mods, g, w_in, lb2)


def _hgrn_scan_kernel(qf_ref, kf_ref, vf_ref, cvf_ref, qb_ref, kb_ref, vb_ref, cvb_ref, s0_ref,
                      of_ref, ob_ref, sfin_ref, st_ref, *, lay, tb, heads):
    d = lay.d
    c = HGRN_C
    dk = d // heads
    i = pl.program_id(0)
    r0 = i * tb
    is_p = r0 < lay.np_rows
    base = jnp.where(is_p, lax.rem(r0, lay.lp), lax.rem(jnp.maximum(r0 - lay.np_rows, 0), lay.ls))
    L = jnp.where(is_p, lay.lp, lay.ls)

    @pl.when(base == 0)
    def _():
        for di in range(2):
            for hd in range(heads):
                s0 = jnp.where(is_p, 0.0, s0_ref[di, hd])
                st_ref[di, hd] = s0.T

    rt = lax.broadcasted_iota(jnp.int32, (c, c), 0)
    cs = lax.broadcasted_iota(jnp.int32, (c, c), 1)
    keep = (cs <= rt, cs >= rt)
    nchunk = tb // c
    dirs = ((qf_ref, kf_ref, vf_ref, cvf_ref, of_ref), (qb_ref, kb_ref, vb_ref, cvb_ref, ob_ref))
    nt_dims = (((1,), (1,)), ((), ()))
    tn_dims = (((0,), (0,)), ((), ()))
    st = [[st_ref[di, hd] for hd in range(heads)] for di in range(2)]
    for n in range(nchunk):
        part = {}
        for di, (q_ref, k_ref, v_ref, cv_ref, _) in enumerate(dirs):
            nn = n if di == 0 else nchunk - 1 - n
            rows = slice(nn * c, (nn + 1) * c)
            for hd in range(heads):
                cols = slice(hd * dk, (hd + 1) * dk)
                qs, ks, vv = q_ref[rows, cols], k_ref[rows, cols], v_ref[rows, cols]
                ku = (ks.astype(F32) * cv_ref[nn, 3 * di + 1:3 * di + 2, cols]).astype(BF16)
                sc = lax.dot_general(qs, ks, nt_dims, preferred_element_type=F32)
                upd = lax.dot_general(vv, ku, tn_dims, preferred_element_type=F32)
                part[di, hd] = (qs, vv, sc, upd)
        for di, (_, _, _, cv_ref, o_ref) in enumerate(dirs):
            nn = n if di == 0 else nchunk - 1 - n
            rows = slice(nn * c, (nn + 1) * c)
            for hd in range(heads):
                cols = slice(hd * dk, (hd + 1) * dk)
                qs, vv, sc, upd = part[di, hd]
                scb = jnp.where(keep[di], sc, 0.0).astype(BF16)
                qi = (qs.astype(F32) * cv_ref[nn, 3 * di:3 * di + 1, cols]).astype(BF16)
                o_ref[rows, cols] = (_dot(scb, vv) + lax.dot_general(
                    qi, st[di][hd].astype(BF16), nt_dims, preferred_element_type=F32)).astype(BF16)
                st[di][hd] = st[di][hd] * cv_ref[nn, 3 * di + 2:3 * di + 3, cols] + upd
    for di in range(2):
        for hd in range(heads):
            st_ref[di, hd] = st[di][hd]

    @pl.when(base + tb == L)
    def _():
        for di in range(2):
            for hd in range(heads):
                sfin_ref[di, hd] = st_ref[di, hd].T


def _hgrn_scan(lay, qf, kf, qb, kb, v, cv, s0, heads):
    tb = HGRN_TB
    d = lay.d
    c = HGRN_C
    dk = d // heads
    lay.check_block(tb)
    assert tb <= lay.lp and tb <= lay.ls
    npb = lay.np_rows // tb
    bps_p, bps_s = lay.lp // tb, lay.ls // tb
    nseq = lay.batch_p + lay.batch_s

    def mirror(i):
        ip = (i // bps_p) * bps_p + (bps_p - 1 - lax.rem(i, bps_p))
        j = jnp.maximum(i - npb, 0)
        isx = npb + (j // bps_s) * bps_s + (bps_s - 1 - lax.rem(j, bps_s))
        return jnp.where(i < npb, ip, isx)

    def seq_of(i):
        return jnp.where(i < npb, i // bps_p, lay.batch_p + jnp.maximum(i - npb, 0) // bps_s)

    fwd = pl.BlockSpec((tb, d), lambda i: (i, 0))
    bwd = pl.BlockSpec((tb, d), lambda i: (mirror(i), 0))
    cvf = pl.BlockSpec((tb // c, 8, d), lambda i: (i, 0, 0))
    cvb = pl.BlockSpec((tb // c, 8, d), lambda i: (mirror(i), 0, 0))
    st_blk = (None, 2, heads, dk, dk)
    s0_spec = pl.BlockSpec(st_blk, lambda i: (jnp.clip(seq_of(i) - lay.batch_p, 0, lay.batch_s - 1), 0, 0, 0, 0))
    sfin_spec = pl.BlockSpec(st_blk, lambda i: (seq_of(i), 0, 0, 0, 0))
    return pl.pallas_call(
        functools.partial(_hgrn_scan_kernel, lay=lay, tb=tb, heads=heads),
        out_shape=[jax.ShapeDtypeStruct((lay.nt, d), BF16), jax.ShapeDtypeStruct((lay.nt, d), BF16),
                   jax.ShapeDtypeStruct((nseq, 2, heads, dk, dk), F32)],
        grid=(lay.nt // tb,),
        in_specs=[fwd, fwd, fwd, cvf, bwd, bwd, bwd, cvb, s0_spec],
        out_specs=[fwd, bwd, sfin_spec],
        scratch_shapes=[pltpu.VMEM((2, heads, dk, dk), F32)],
        compiler_params=_cparams(("arbitrary",)),
        name="hgrn_scan",
    )(qf, kf, v, cv, qb, kb, v, cv, s0)


def _hgrn_out_kernel(x_ref, of_ref, ob_ref, sg_ref, mod_ref, ng_ref, wout_ref, o_ref, *, lay, heads):
    d = lay.d
    dk = d // heads
    _, _, gate = _mod_parts(mod_ref, d, first=True)
    o = of_ref[...].astype(F32) + ob_ref[...].astype(F32)
    parts = []
    for hd in range(heads):
        oh = o[:, hd * dk:(hd + 1) * dk]
        ms = jnp.mean(oh * oh, axis=-1, keepdims=True)
        parts.append(oh * lax.rsqrt(ms + RMS_EPS))
    on = jnp.concatenate(parts, axis=-1) * ng_ref[...]
    y = _dot((on * sg_ref[...].astype(F32)).astype(BF16), wout_ref[...])
    o_ref[...] = x_ref[...] + gate * y


def _hgrn_out(lay, x, o_f, o_b, sg, mods, layer, norm_g, j, w_out, heads):
    tb = TB
    d = lay.d
    blk = pl.BlockSpec((tb, d), lambda i: (i, 0))
    return pl.pallas_call(
        functools.partial(_hgrn_out_kernel, lay=lay, heads=heads),
        out_shape=jax.ShapeDtypeStruct((lay.nt, d), F32),
        grid=(lay.nt // tb,),
        in_specs=[blk, blk, blk, blk, _mod_spec(lay, layer, tb), _const_spec((1, d)), _layer_spec(w_out, j)],
        out_specs=blk,
        compiler_params=_cparams(("parallel",)),
        name="hgrn_out",
    )(x, o_f, o_b, sg, mods, norm_g, w_out)


def kernel(x_prompt, x_sample, state_rec, c, c_ctx, ada_w, ada_b, norm_g, final_g, conv_w_in, conv_w_dw,
           conv_w_out, pool_w, pool_scale, sgu_w_in, sgu_norm_g, sgu_w_s, sgu_b_s, sgu_w_out, hgrn_w_in,
           hgrn_lb, hgrn_norm_g, hgrn_w_out, ffn_w_up, ffn_w_dw, ffn_w_down):
    bp, lp, d = x_prompt.shape
    bs, ls, _ = x_sample.shape
    depth = ada_w.shape[0]
    heads = state_rec.shape[3]
    lay = _Layout(bp, lp, bs, ls, d)
    lay.check_block(TB)
    assert c.shape[0] == bs and 1 + bs <= 8

    cvec = jnp.concatenate([c_ctx[None], c, jnp.zeros((8 - 1 - bs, d), F32)], axis=0)
    mods = _ada_params(cvec, ada_w, ada_b).reshape(depth * 8, 1, 6 * d)

    assert N_MIXERS >= 1 and depth >= 1
    raw = (x_prompt.reshape(bp * lp, d), x_sample.reshape(bs * ls, d), _pos_table(ls, d))
    x = None

    wb = {k: v.astype(BF16) for k, v in dict(
        conv_in=conv_w_in, conv_out=conv_w_out, pool=pool_w, sgu_in=sgu_w_in, sgu_s=sgu_w_s, sgu_out=sgu_w_out,
        hgrn_in=hgrn_w_in, hgrn_out=hgrn_w_out, ffn_up=ffn_w_up, ffn_down=ffn_w_down).items()}

    new_states = []
    for i in range(depth):
        kind, j = i % N_MIXERS, i // N_MIXERS
        g1 = norm_g[i, 0].reshape(1, d)
        g2 = norm_g[i, 1].reshape(1, d)
        if kind == 0:
            x = _sconv_layer(lay, x, mods, i, g1, j, wb["conv_in"], conv_w_dw, wb["conv_out"],
                             raw=raw if i == 0 else None)
        elif kind == 1:
            x = _pool_layer(lay, x, mods, i, g1, j, wb["pool"], pool_scale[j].reshape(1, d))
        elif kind == 2:
            x = _sgu_layer(lay, x, mods, i, g1, j, wb["sgu_in"], sgu_norm_g[j].reshape(1, d),
                           wb["sgu_s"], sgu_b_s[j].T, wb["sgu_out"])
        else:
            qf, kf, qb, kb, v, sg, cv = _hgrn_proj(lay, x, mods, i, g1, j, wb["hgrn_in"],
                                                   hgrn_lb.reshape(depth, 2 * d))
            o_f, o_b, sfin = _hgrn_scan(lay, qf, kf, qb, kb, v, cv, state_rec[:, j], heads)
            new_states.append(sfin[:bp])
            x = _hgrn_out(lay, x, o_f, o_b, sg, mods, i, hgrn_norm_g[j].reshape(1, d), j, wb["hgrn_out"], heads)
        if i + 1 < depth:
            x = _ffn(lay, x, mods, i, g2, wb["ffn_up"], ffn_w_dw, wb["ffn_down"])
        else:
            y_prompt, y_sample = _ffn(lay, x, mods, i, g2, wb["ffn_up"], ffn_w_dw, wb["ffn_down"],
                                      final_g.reshape(1, d))
    y_prompt = y_prompt.reshape(bp, lp, d)
    y_sample = y_sample.reshape(bs, ls, d)
    new_state_rec = jnp.stack(new_states, axis=1)
    return (y_prompt, y_sample, new_state_rec)
```

```python
import functools
import math

import jax
import jax.numpy as jnp
from jax import lax
from jax.experimental import pallas as pl
from jax.experimental.pallas import tpu as pltpu

GRID_W = 64
N_MIXERS = 4
RMS_EPS = 1e-6
POOL_WINDOWS = (2, 4, 8, 16)
SGU_CHUNK = 128
POS_BASE = 10000.0

HALO = 8
TB = 512
FFN_TB = 256
FFN_FC = 256
HGRN_C = 64
HGRN_TB = 256
EXP_CLAMP = 80.0
F32_TINY = 1e-37
VMEM_LIMIT = 60000 * 1024

BF16 = jnp.bfloat16
F32 = jnp.float32


def _cparams(sem):
    return pltpu.CompilerParams(dimension_semantics=sem, vmem_limit_bytes=VMEM_LIMIT)


def _dot(a, b):
    return jnp.dot(a, b, preferred_element_type=F32)


def _const_spec(shape):
    nd = len(shape)
    return pl.BlockSpec(shape, lambda i: (0,) * nd)


def _layer_spec(stacked, idx):
    rest = stacked.shape[1:]
    return pl.BlockSpec((None,) + rest, lambda i: (idx,) + (0,) * len(rest))


class _Layout:
    def __init__(self, batch_p, seq_p, batch_s, seq_s, d):
        self.lp, self.ls, self.d = seq_p, seq_s, d
        self.np_rows = batch_p * seq_p
        self.ns_rows = batch_s * seq_s
        self.nt = self.np_rows + self.ns_rows
        self.batch_p, self.batch_s = batch_p, batch_s

    def check_block(self, tb):
        assert self.np_rows % tb == 0 and self.ns_rows % tb == 0
        for L in (self.lp, self.ls):
            assert L % tb == 0 or tb % L == 0

    def cond_row(self, r0):
        return jnp.where(r0 < self.np_rows, 0, 1 + (jnp.maximum(r0 - self.np_rows, 0)) // self.ls)


def _pos_in_seq(lay, r0, tb):
    k = lax.broadcasted_iota(jnp.int32, (tb, 1), 0)

    def pos_for(start, L):
        base = lax.rem(jnp.maximum(r0 - start, 0), L)
        p = base + k
        for j in range(1, (tb + L - 1) // L + 1):
            p = p - jnp.where(base + k >= j * L, L, 0)
        return p

    is_p = r0 < lay.np_rows
    pos = jnp.where(is_p, pos_for(0, lay.lp), pos_for(lay.np_rows, lay.ls))
    L = jnp.where(is_p, lay.lp, lay.ls)
    return pos, L


def _rms_mod(x, g, shift, scale):
    ms = jnp.mean(x * x, axis=-1, keepdims=True)
    return x * lax.rsqrt(ms + RMS_EPS) * (g * (1.0 + scale)) + shift


def _mod_parts(mod_ref, d, first):
    o = 0 if first else 3 * d
    return (mod_ref[:, o:o + d], mod_ref[:, o + d:o + 2 * d], mod_ref[:, o + 2 * d:o + 3 * d])


def _halo_specs(lay, tb):
    nb8 = lay.nt // HALO
    r = tb // HALO
    return [
        pl.BlockSpec((tb, lay.d), lambda i: (i, 0)),
        pl.BlockSpec((HALO, lay.d), lambda i: (jnp.maximum(i * r - 1, 0), 0)),
        pl.BlockSpec((HALO, lay.d), lambda i: (jnp.minimum((i + 1) * r, nb8 - 1), 0)),
    ]


def _mod_spec(lay, layer, tb):
    return pl.BlockSpec((None, 1, 6 * lay.d), lambda i: (layer * 8 + lay.cond_row(i * tb), 0, 0))


def _edge_groups(lay, tb):
    step = math.gcd(math.gcd(lay.lp, lay.ls), tb)
    assert step % HALO == 0
    firsts = list(range(0, tb, step))
    lasts = [(k - HALO) % tb for k in firsts]
    return firsts, lasts


def _zero_rows(a, mask, groups):
    pieces, cur = [], 0
    for r in sorted(groups):
        if r > cur:
            pieces.append(a[cur:r])
        pieces.append(jnp.where(mask[r:r + HALO], 0.0, a[r:r + HALO]))
        cur = r + HALO
    if cur < a.shape[0]:
        pieces.append(a[cur:])
    return jnp.concatenate(pieces, axis=0)


def _dwconv3(a_ref, w, first, last, tb, edges):
    prev = a_ref[HALO - 1:HALO - 1 + tb, :]
    mid = a_ref[HALO:HALO + tb, :]
    nxt = a_ref[HALO + 1:HALO + 1 + tb, :]
    prev = _zero_rows(prev, first, edges[0])
    nxt = _zero_rows(nxt, last, edges[1])
    return prev * w[0:1] + mid * w[1:2] + nxt * w[2:3]


def _pos_kernel(o_ref, *, tb, d):
    i = pl.program_id(0)
    q = d // 4
    nr = tb // GRID_W
    j = lax.broadcasted_iota(jnp.int32, (1, q), 1).astype(F32)
    freq = jnp.exp(-math.log(POS_BASE) * j / q)
    rr = (i * nr + lax.broadcasted_iota(jnp.int32, (nr, 1), 0)).astype(F32)
    cc = lax.broadcasted_iota(jnp.int32, (GRID_W, 1), 0).astype(F32)
    ar = rr * freq
    ac = cc * freq
    row_part = jnp.concatenate([jnp.sin(ar), jnp.cos(ar)], axis=1)
    col_part = jnp.concatenate([jnp.sin(ac), jnp.cos(ac)], axis=1)
    for r in range(nr):
        rows = slice(r * GRID_W, (r + 1) * GRID_W)
        o_ref[rows, 0:2 * q] = jnp.broadcast_to(row_part[r:r + 1], (GRID_W, 2 * q))
        o_ref[rows, 2 * q:4 * q] = col_part


def _pos_table(n_tokens, d):
    tb = 512
    assert tb % GRID_W == 0 and n_tokens % tb == 0
    return pl.pallas_call(
        functools.partial(_pos_kernel, tb=tb, d=d),
        out_shape=jax.ShapeDtypeStruct((n_tokens, d), F32),
        grid=(n_tokens // tb,),
        out_specs=pl.BlockSpec((tb, d), lambda i: (i, 0)),
        compiler_params=_cparams(("parallel",)),
        name="pos_table",
    )()


def _ada_kernel(c_ref, w_ref, b_ref, o_ref):
    c = c_ref[...]
    s = (c * jax.nn.sigmoid(c)).astype(BF16)
    o_ref[...] = _dot(s, w_ref[...].astype(BF16)) + b_ref[...]


def _ada_params(cvec, ada_w, ada_b):
    depth, d, n = ada_w.shape
    tn = 1536
    assert n % tn == 0
    return pl.pallas_call(
        _ada_kernel,
        out_shape=jax.ShapeDtypeStruct((depth, 8, n), F32),
        grid=(depth, n // tn),
        in_specs=[
            pl.BlockSpec((8, d), lambda l, j: (0, 0)),
            pl.BlockSpec((None, d, tn), lambda l, j: (l, 0, j)),
            pl.BlockSpec((None, 1, tn), lambda l, j: (l, 0, j)),
        ],
        out_specs=pl.BlockSpec((None, 8, tn), lambda l, j: (l, 0, j)),
        compiler_params=_cparams(("parallel", "parallel")),
        name="ada_params",
    )(cvec, ada_w, ada_b.reshape(depth, 1, n))


def _edge_points(lay, r0, tb):
    is_p = r0 < lay.np_rows
    base = jnp.where(is_p, lax.rem(r0, lay.lp), lax.rem(jnp.maximum(r0 - lay.np_rows, 0), lay.ls))
    L = jnp.where(is_p, lay.lp, lay.ls)
    step = math.gcd(math.gcd(lay.lp, lay.ls), tb)
    firsts = [(k, lax.rem(base + k, L) == 0) for k in range(0, tb, step)]
    lasts = [((k - 1) % tb, lax.rem(base + ((k - 1) % tb) + 1, L) == 0) for k in range(0, tb, step)]
    return firsts, lasts


def _zero_points(a, points, s_rows):
    sub = lax.broadcasted_iota(jnp.int32, (HALO, 1), 0)
    pieces, cur = [], 0
    for k, flag in sorted(points, key=lambda p: (p[0] + HALO) % s_rows):
        s, j = divmod(k + HALO, s_rows)
        r = HALO * j
        assert r >= cur
        if r > cur:
            pieces.append(a[cur:r])
        pieces.append(jnp.where(jnp.logical_and(sub == s, flag), 0.0, a[r:r + HALO]))
        cur = r + HALO
    if cur < a.shape[0]:
        pieces.append(a[cur:])
    return jnp.concatenate(pieces, axis=0)


def _dwconv3_strided(u_ref, w, firsts, lasts):
    n = u_ref.shape[0]
    s_rows = n // HALO
    mid = u_ref[...]
    prev = jnp.concatenate([pltpu.roll(u_ref[n - HALO:n, :], 1, axis=0), u_ref[0:n - HALO, :]], axis=0)
    nxt = jnp.concatenate([u_ref[HALO:n, :], pltpu.roll(u_ref[0:HALO, :], HALO - 1, axis=0)], axis=0)
    prev = _zero_points(prev, firsts, s_rows)
    nxt = _zero_points(nxt, lasts, s_rows)
    return prev * w[0:1] + mid * w[1:2] + nxt * w[2:3]


def _ffn_kernel(*refs, lay, tb, nf, final):
    if final:
        (x_ref, xp_ref, xn_ref, mod_ref, g_ref, wup_ref, wdw_ref, wdn_ref, fg_ref,
         yp_ref, ys_ref, slab_ref, u_ref) = refs
    else:
        x_ref, xp_ref, xn_ref, mod_ref, g_ref, wup_ref, wdw_ref, wdn_ref, o_ref, slab_ref, u_ref = refs
    d = lay.d
    fc = FFN_FC
    n = tb + 2 * HALO
    s_rows = n // HALO
    nl = d // 128
    i = pl.program_id(0)
    firsts, lasts = _edge_points(lay, i * tb, tb)
    shift, scale, gate = _mod_parts(mod_ref, d, first=False)

    for l in range(nl):
        cols = slice(128 * l, 128 * (l + 1))
        slab_ref[l, 0:HALO, :] = xp_ref[:, cols]
        slab_ref[l, HALO:HALO + tb, :] = x_ref[:, cols]
        slab_ref[l, HALO + tb:n, :] = xn_ref[:, cols]
    xs = jnp.concatenate(
        [jnp.concatenate([slab_ref[l, pl.ds(j, HALO, stride=s_rows), :] for l in range(nl)], axis=1)
         for j in range(s_rows)], axis=0)
    h = _rms_mod(xs, g_ref[...], shift, scale).astype(BF16)

    def up(c):
        for ab in range(2):
            u_ref[c % 2, ab] = _dot(h, wup_ref[:, ab * nf * fc + c * fc:ab * nf * fc + (c + 1) * fc])

    up(0)
    acc = None
    for c in range(nf):
        if c + 1 < nf:
            up(c + 1)
        ca = _dwconv3_strided(u_ref.at[c % 2, 0], wdw_ref[:, c * fc:(c + 1) * fc], firsts, lasts)
        cb = _dwconv3_strided(u_ref.at[c % 2, 1], wdw_ref[:, (nf + c) * fc:(nf + c + 1) * fc], firsts, lasts)
        act = (ca * jax.nn.sigmoid(ca) * cb).astype(BF16)
        part = _dot(act, wdn_ref[c * fc:(c + 1) * fc, :])
        acc = part if acc is None else acc + part

    for l in range(nl):
        for j in range(s_rows):
            slab_ref[l, pl.ds(j, HALO, stride=s_rows), :] = acc[HALO * j:HALO * (j + 1), 128 * l:128 * (l + 1)]
    y = jnp.concatenate([slab_ref[l, HALO:HALO + tb, :] for l in range(nl)], axis=1)
    out = x_ref[...] + gate * y
    if not final:
        o_ref[...] = out
    else:
        ms = jnp.mean(out * out, axis=-1, keepdims=True)
        out = out * lax.rsqrt(ms + RMS_EPS) * fg_ref[...]
        is_p = i * tb < lay.np_rows

        @pl.when(is_p)
        def _():
            yp_ref[...] = out

        @pl.when(jnp.logical_not(is_p))
        def _():
            ys_ref[...] = out


def _ffn(lay, x, mods, layer, g, wup, wdw, wdn, final_g=None):
    tb = FFN_TB
    d = lay.d
    assert wdn.shape[1] % FFN_FC == 0
    nf = wdn.shape[1] // FFN_FC
    n = tb + 2 * HALO
    lay.check_block(tb)
    assert n % HALO == 0 and d % 128 == 0
    final = final_g is not None
    blk = pl.BlockSpec((tb, d), lambda i: (i, 0))
    if final:
        npb = lay.np_rows // tb
        out_shape = [jax.ShapeDtypeStruct((lay.np_rows, d), F32), jax.ShapeDtypeStruct((lay.ns_rows, d), F32)]
        out_specs = [pl.BlockSpec((tb, d), lambda i: (jnp.minimum(i, npb - 1), 0)),
                     pl.BlockSpec((tb, d), lambda i: (jnp.maximum(i - npb, 0), 0))]
    else:
        out_shape = jax.ShapeDtypeStruct((lay.nt, d), F32)
        out_specs = blk
    return pl.pallas_call(
        functools.partial(_ffn_kernel, lay=lay, tb=tb, nf=nf, final=final),
        out_shape=out_shape,
        grid=(lay.nt // tb,),
        in_specs=_halo_specs(lay, tb) + [
            _mod_spec(lay, layer, tb),
            _const_spec((1, d)),
            _layer_spec(wup, layer),
            _layer_spec(wdw, layer),
            _layer_spec(wdn, layer),
        ] + ([_const_spec((1, d))] if final else []),
        out_specs=out_specs,
        scratch_shapes=[pltpu.VMEM((d // 128, n, 128), F32),
                        pltpu.VMEM((2, 2, n, FFN_FC), F32)],
        compiler_params=_cparams(("arbitrary",)),
        name="conv_ffn",
    )(x, x, x, mods, g, wup, wdw, wdn, *([final_g] if final else []))


def _sconv_kernel(*refs, lay, tb, embed):
    if embed:
        (xp_ref, xpp_ref, xpn_ref, xs_ref, xsp_ref, xsn_ref, ps_ref, psp_ref, psn_ref,
         mod_ref, g_ref, win_ref, wdw_ref, wout_ref, o_ref, xh_ref, p_ref) = refs
    else:
        x_ref, xp_ref, xn_ref, mod_ref, g_ref, win_ref, wdw_ref, wout_ref, o_ref, xh_ref, p_ref = refs
    d = lay.d
    n = tb + 2 * HALO
    i = pl.program_id(0)
    pos, L = _pos_in_seq(lay, i * tb, tb)
    first, last = pos == 0, pos == L - 1
    shift, scale, gate = _mod_parts(mod_ref, d, first=True)
    if embed:
        is_p = i * tb < lay.np_rows

        @pl.when(is_p)
        def _():
            xh_ref[0:HALO, :] = xpp_ref[...]
            xh_ref[HALO:HALO + tb, :] = xp_ref[...]
            xh_ref[HALO + tb:n, :] = xpn_ref[...]

        @pl.when(jnp.logical_not(is_p))
        def _():
            xh_ref[0:HALO, :] = xsp_ref[...] + psp_ref[...]
            xh_ref[HALO:HALO + tb, :] = xs_ref[...] + ps_ref[...]
            xh_ref[HALO + tb:n, :] = xsn_ref[...] + psn_ref[...]
    else:
        xh_ref[0:HALO, :] = xp_ref[...]
        xh_ref[HALO:HALO + tb, :] = x_ref[...]
        xh_ref[HALO + tb:n, :] = xn_ref[...]
    h = _rms_mod(xh_ref[...], g_ref[...], shift, scale).astype(BF16)
    bg = _dot(h, win_ref[:, 0:d])[HALO:HALO + tb]
    p_ref[...] = _dot(h, win_ref[:, d:2 * d]) * _dot(h, win_ref[:, 2 * d:3 * d])
    conv = _dwconv3(p_ref, wdw_ref[...], first, last, tb, _edge_groups(lay, tb))
    y = _dot((bg * conv).astype(BF16), wout_ref[...])
    o_ref[...] = xh_ref[HALO:HALO + tb, :] + gate * y


def _stream_halo_specs(rows, d, tb, blk_of):
    r = tb // HALO
    last8 = rows // HALO - 1
    return [
        pl.BlockSpec((tb, d), lambda i: (blk_of(i), 0)),
        pl.BlockSpec((HALO, d), lambda i: (jnp.clip(blk_of(i) * r - 1, 0, last8), 0)),
        pl.BlockSpec((HALO, d), lambda i: (jnp.clip((blk_of(i) + 1) * r, 0, last8), 0)),
    ]


def _sconv_layer(lay, x, mods, layer, g, j, w_in, w_dw, w_out, raw=None):
    tb = TB
    d = lay.d
    embed = raw is not None
    if embed:
        npb = lay.np_rows // tb
        psb = lay.ls // tb
        xspecs = (_stream_halo_specs(lay.np_rows, d, tb, lambda i: jnp.minimum(i, npb - 1))
                  + _stream_halo_specs(lay.ns_rows, d, tb, lambda i: jnp.maximum(i - npb, 0))
                  + _stream_halo_specs(lay.ls, d, tb, lambda i: lax.rem(jnp.maximum(i - npb, 0), psb)))
        xargs = (raw[0],) * 3 + (raw[1],) * 3 + (raw[2],) * 3
    else:
        xspecs = _halo_specs(lay, tb)
        xargs = (x, x, x)
    return pl.pallas_call(
        functools.partial(_sconv_kernel, lay=lay, tb=tb, embed=embed),
        out_shape=jax.ShapeDtypeStruct((lay.nt, d), F32),
        grid=(lay.nt // tb,),
        in_specs=xspecs + [
            _mod_spec(lay, layer, tb),
            _const_spec((1, d)),
            _layer_spec(w_in, j),
            _layer_spec(w_dw, j),
            _layer_spec(w_out, j),
        ],
        out_specs=pl.BlockSpec((tb, d), lambda i: (i, 0)),
        scratch_shapes=[pltpu.VMEM((tb + 2 * HALO, d), F32), pltpu.VMEM((tb + 2 * HALO, d), F32)],
        compiler_params=_cparams(("arbitrary",)),
        name="sconv_mixer",
    )(*xargs, mods, g, w_in, w_dw, w_out)


def _pool_kernel(x_ref, xp_ref, xn_ref, mod_ref, g_ref, w_ref, sc_ref, o_ref, h_ref, *, lay, tb):
    d = lay.d
    ng = len(POOL_WINDOWS)
    pg = d // ng
    i = pl.program_id(0)
    pos, L = _pos_in_seq(lay, i * tb, tb)
    shift, scale, gate = _mod_parts(mod_ref, d, first=True)
    xh = jnp.concatenate([xp_ref[...], x_ref[...], xn_ref[...]], axis=0)
    h_ref[...] = _rms_mod(xh, g_ref[...], shift, scale)
    n = tb + 2 * HALO
    edge_groups = sorted(set(_edge_groups(lay, tb)[0]) | set(_edge_groups(lay, tb)[1]))
    outs = []
    for gi, w in enumerate(POOL_WINDOWS):
        assert w // 2 <= HALO and w & (w - 1) == 0
        cols = slice(gi * pg, (gi + 1) * pg)
        centre = h_ref[HALO:HALO + tb, cols]
        p = h_ref[:, cols]
        m = 1
        while m < w:
            p = p + pltpu.roll(p, m, axis=0)
            m *= 2
        ahead = w // 2 - 1
        if ahead:
            p = pltpu.roll(p, n - ahead, axis=0)
        mean = p[HALO:HALO + tb] * (1.0 / w)
        pieces, cur = [], 0
        for r in edge_groups:
            if r > cur:
                pieces.append(mean[cur:r])
            pr = pos[r:r + HALO]
            s = jnp.zeros((HALO, pg), F32)
            cnt = jnp.zeros((HALO, 1), F32)
            for k in range(-(w // 2), w // 2):
                inside = jnp.logical_and(pr + k >= 0, pr + k <= L - 1)
                s = s + jnp.where(inside, h_ref[HALO + r + k:HALO + r + k + HALO, cols], 0.0)
                cnt = cnt + jnp.where(inside, 1.0, 0.0)
            pieces.append(s / cnt)
            cur = r + HALO
        if cur < tb:
            pieces.append(mean[cur:])
        mean = jnp.concatenate(pieces, axis=0)
        pooled = (mean - centre).astype(BF16)
        outs.append(_dot(pooled, w_ref[gi]))
    y = jnp.concatenate(outs, axis=-1) * sc_ref[...]
    o_ref[...] = x_ref[...] + gate * y


def _pool_layer(lay, x, mods, layer, g, j, w, sc):
    tb = TB
    d = lay.d
    return pl.pallas_call(
        functools.partial(_pool_kernel, lay=lay, tb=tb),
        out_shape=jax.ShapeDtypeStruct((lay.nt, d), F32),
        grid=(lay.nt // tb,),
        in_specs=_halo_specs(lay, tb) + [
            _mod_spec(lay, layer, tb),
            _const_spec((1, d)),
            _layer_spec(w, j),
            _const_spec((1, d)),
        ],
        out_specs=pl.BlockSpec((tb, d), lambda i: (i, 0)),
        scratch_shapes=[pltpu.VMEM((tb + 2 * HALO, d), F32)],
        compiler_params=_cparams(("parallel",)),
        name="pool_mixer",
    )(x, x, x, mods, g, w, sc)


def _sgu_kernel(x_ref, mod_ref, g_ref, win_ref, ng_ref, ws_ref, bs_ref, wout_ref, o_ref, s_ref, *, lay, tb):
    d = lay.d
    groups = ws_ref.shape[0]
    gd = d // groups
    shift, scale, gate = _mod_parts(mod_ref, d, first=True)
    x = x_ref[...]
    h = _rms_mod(x, g_ref[...], shift, scale).astype(BF16)
    v = jax.nn.gelu(_dot(h, win_ref[:, d:2 * d]), approximate=True)
    u = jax.nn.gelu(_dot(h, win_ref[:, 0:d]), approximate=True)
    ms = jnp.mean(v * v, axis=-1, keepdims=True)
    vb = (v * lax.rsqrt(ms + RMS_EPS) * ng_ref[...]).astype(BF16)
    for n in range(tb // SGU_CHUNK):
        for gi in range(groups):
            rows = slice(n * SGU_CHUNK, (n + 1) * SGU_CHUNK)
            cols = slice(gi * gd, (gi + 1) * gd)
            s_ref[rows, cols] = _dot(ws_ref[gi], vb[rows, cols]) + bs_ref[:, gi:gi + 1]
    y = _dot((u * s_ref[...]).astype(BF16), wout_ref[...])
    o_ref[...] = x + gate * y


def _sgu_layer(lay, x, mods, layer, g, j, w_in, norm_g, w_s, b_st, w_out):
    tb = TB
    d = lay.d
    assert tb % SGU_CHUNK == 0 and lay.lp % SGU_CHUNK == 0 and lay.ls % SGU_CHUNK == 0
    return pl.pallas_call(
        functools.partial(_sgu_kernel, lay=lay, tb=tb),
        out_shape=jax.ShapeDtypeStruct((lay.nt, d), F32),
        grid=(lay.nt // tb,),
        in_specs=[
            pl.BlockSpec((tb, d), lambda i: (i, 0)),
            _mod_spec(lay, layer, tb),
            _const_spec((1, d)),
            _layer_spec(w_in, j),
            _const_spec((1, d)),
            _layer_spec(w_s, j),
            _const_spec(b_st.shape),
            _layer_spec(w_out, j),
        ],
        out_specs=pl.BlockSpec((tb, d), lambda i: (i, 0)),
        scratch_shapes=[pltpu.VMEM((tb, d), F32)],
        compiler_params=_cparams(("parallel",)),
        name="sgu_mixer",
    )(x, mods, g, w_in, norm_g, w_s, b_st, w_out)


def _split2(a):
    hi = a.astype(BF16)
    lo = (a - hi.astype(F32)).astype(BF16)
    return hi, lo


def _hgrn_gates(z, lb, one_m_lb):
    ez = jnp.exp(-jnp.abs(z))
    r = 1.0 / (1.0 + ez)
    er = ez * r
    pos = z >= 0.0
    f = lb + one_m_lb * jnp.where(pos, r, er)
    logf = jnp.log(jnp.maximum(f, F32_TINY))
    return logf, one_m_lb * jnp.where(pos, er, r)


def _hgrn_proj_kernel(x_ref, mod_ref, g_ref, win_ref, lb_ref,
                      qf_ref, kf_ref, qb_ref, kb_ref, v_ref, sg_ref, cv_ref, *, lay, tb, layer):
    d = lay.d
    c = HGRN_C
    shift, scale, _ = _mod_parts(mod_ref, d, first=True)
    h = _rms_mod(x_ref[...], g_ref[...], shift, scale).astype(BF16)

    lbp = lb_ref[...]
    e = jnp.exp(lbp - jnp.max(lbp, axis=0, keepdims=True))
    p = e / jnp.sum(e, axis=0, keepdims=True)
    lb = jnp.zeros((1, 2 * d), F32)
    for j in range(1, layer + 1):
        lb = lb + p[j:j + 1]
    one_m_lb = 1.0 - lb

    z = [_dot(h, win_ref[:, (1 + di) * d:(2 + di) * d]) for di in range(2)]
    q = _dot(h, win_ref[:, 0:d])
    gates = [_hgrn_gates(z[di], lb[:, di * d:(di + 1) * d], one_m_lb[:, di * d:(di + 1) * d]) for di in range(2)]

    rt = lax.broadcasted_iota(jnp.int32, (c, 2 * c), 0)
    cs = lax.broadcasted_iota(jnp.int32, (c, 2 * c), 1)
    cs = jnp.where(cs >= c, cs - c, cs)
    tri_f = (cs <= rt).astype(BF16)
    tri_b = (cs >= rt).astype(BF16)
    mid = c // 2

    def cumsum(di, tri):
        logf = gates[di][0]
        return [_dot(tri, jnp.concatenate(_split2(logf[n * c:(n + 1) * c]), axis=0)) for n in range(tb // c)]

    cum = [cumsum(0, tri_f)]
    vv = _dot(h, win_ref[:, 3 * d:4 * d])
    cum.append(cumsum(1, tri_b))
    gg = _dot(h, win_ref[:, 4 * d:5 * d])

    for di, (q_ref, k_ref) in enumerate(((qf_ref, kf_ref), (qb_ref, kb_ref))):
        kk = gates[di][1]
        for n in range(tb // c):
            rows = slice(n * c, (n + 1) * c)
            b = cum[di][n]
            if di == 0:
                btot = b[c - 1:c]
                bref = b[mid - 1:mid]
            else:
                btot = b[0:1]
                bref = b[mid:mid + 1]
            q_ref[rows, :] = (q[rows] * jnp.exp(jnp.minimum(b - bref, EXP_CLAMP))).astype(BF16)
            k_ref[rows, :] = (kk[rows] * jnp.exp(jnp.minimum(bref - b, EXP_CLAMP))).astype(BF16)
            cv_ref[n, 3 * di:3 * di + 1, :] = jnp.exp(bref)
            cv_ref[n, 3 * di + 1:3 * di + 2, :] = jnp.exp(btot - bref)
            cv_ref[n, 3 * di + 2:3 * di + 3, :] = jnp.exp(btot)
    v_ref[...] = vv.astype(BF16)
    sg_ref[...] = (gg * jax.nn.sigmoid(gg)).astype(BF16)
    for n in range(tb // c):
        cv_ref[n, 6:8, :] = jnp.zeros((2, d), F32)


def _hgrn_proj(lay, x, mods, layer, g, j, w_in, lb2):
    tb = TB
    d = lay.d
    c = HGRN_C
    blk = pl.BlockSpec((tb, d), lambda i: (i, 0))
    act = jax.ShapeDtypeStruct((lay.nt, d), BF16)
    return pl.pallas_call(
        functools.partial(_hgrn_proj_kernel, lay=lay, tb=tb, layer=layer),
        out_shape=[act, act, act, act, act, act, jax.ShapeDtypeStruct((lay.nt // c, 8, d), F32)],
        grid=(lay.nt // tb,),
        in_specs=[blk, _mod_spec(lay, layer, tb), _const_spec((1, d)), _layer_spec(w_in, j),
                  _const_spec(lb2.shape)],
        out_specs=[blk, blk, blk, blk, blk, blk, pl.BlockSpec((tb // c, 8, d), lambda i: (i, 0, 0))],
        compiler_params=_cparams(("parallel",)),
        name="hgrn_proj",
    )(x, mods, g, w_in, lb2)


def _hgrn_scan_kernel(qf_ref, kf_ref, vf_ref, cvf_ref, qb_ref, kb_ref, vb_ref, cvb_ref, s0_ref,
                      of_ref, ob_ref, sfin_ref, st_ref, *, lay, tb, heads):
    d = lay.d
    c = HGRN_C
    dk = d // heads
    i = pl.program_id(0)
    r0 = i * tb
    is_p = r0 < lay.np_rows
    base = jnp.where(is_p, lax.rem(r0, lay.lp), lax.rem(jnp.maximum(r0 - lay.np_rows, 0), lay.ls))
    L = jnp.where(is_p, lay.lp, lay.ls)

    @pl.when(base == 0)
    def _():
        for di in range(2):
            for hd in range(heads):
                s0 = jnp.where(is_p, 0.0, s0_ref[di, hd])
                st_ref[di, hd] = s0.T

    rt = lax.broadcasted_iota(jnp.int32, (c, c), 0)
    cs = lax.broadcasted_iota(jnp.int32, (c, c), 1)
    keep = (cs <= rt, cs >= rt)
    nchunk = tb // c
    dirs = ((qf_ref, kf_ref, vf_ref, cvf_ref, of_ref), (qb_ref, kb_ref, vb_ref, cvb_ref, ob_ref))
    nt_dims = (((1,), (1,)), ((), ()))
    tn_dims = (((0,), (0,)), ((), ()))
    st = [[st_ref[di, hd] for hd in range(heads)] for di in range(2)]
    for n in range(nchunk):
        part = {}
        for di, (q_ref, k_ref, v_ref, cv_ref, _) in enumerate(dirs):
            nn = n if di == 0 else nchunk - 1 - n
            rows = slice(nn * c, (nn + 1) * c)
            for hd in range(heads):
                cols = slice(hd * dk, (hd + 1) * dk)
                qs, ks, vv = q_ref[rows, cols], k_ref[rows, cols], v_ref[rows, cols]
                ku = (ks.astype(F32) * cv_ref[nn, 3 * di + 1:3 * di + 2, cols]).astype(BF16)
                sc = lax.dot_general(qs, ks, nt_dims, preferred_element_type=F32)
                upd = lax.dot_general(vv, ku, tn_dims, preferred_element_type=F32)
                part[di, hd] = (qs, vv, sc, upd)
        for di, (_, _, _, cv_ref, o_ref) in enumerate(dirs):
            nn = n if di == 0 else nchunk - 1 - n
            rows = slice(nn * c, (nn + 1) * c)
            for hd in range(heads):
                cols = slice(hd * dk, (hd + 1) * dk)
                qs, vv, sc, upd = part[di, hd]
                scb = jnp.where(keep[di], sc, 0.0).astype(BF16)
                qi = (qs.astype(F32) * cv_ref[nn, 3 * di:3 * di + 1, cols]).astype(BF16)
                o_ref[rows, cols] = (_dot(scb, vv) + lax.dot_general(
                    qi, st[di][hd].astype(BF16), nt_dims, preferred_element_type=F32)).astype(BF16)
                st[di][hd] = st[di][hd] * cv_ref[nn, 3 * di + 2:3 * di + 3, cols] + upd
    for di in range(2):
        for hd in range(heads):
            st_ref[di, hd] = st[di][hd]

    @pl.when(base + tb == L)
    def _():
        for di in range(2):
            for hd in range(heads):
                sfin_ref[di, hd] = st_ref[di, hd].T


def _hgrn_scan(lay, qf, kf, qb, kb, v, cv, s0, heads):
    tb = HGRN_TB
    d = lay.d
    c = HGRN_C
    dk = d // heads
    lay.check_block(tb)
    assert tb <= lay.lp and tb <= lay.ls
    npb = lay.np_rows // tb
    bps_p, bps_s = lay.lp // tb, lay.ls // tb
    nseq = lay.batch_p + lay.batch_s

    def mirror(i):
        ip = (i // bps_p) * bps_p + (bps_p - 1 - lax.rem(i, bps_p))
        j = jnp.maximum(i - npb, 0)
        isx = npb + (j // bps_s) * bps_s + (bps_s - 1 - lax.rem(j, bps_s))
        return jnp.where(i < npb, ip, isx)

    def seq_of(i):
        return jnp.where(i < npb, i // bps_p, lay.batch_p + jnp.maximum(i - npb, 0) // bps_s)

    fwd = pl.BlockSpec((tb, d), lambda i: (i, 0))
    bwd = pl.BlockSpec((tb, d), lambda i: (mirror(i), 0))
    cvf = pl.BlockSpec((tb // c, 8, d), lambda i: (i, 0, 0))
    cvb = pl.BlockSpec((tb // c, 8, d), lambda i: (mirror(i), 0, 0))
    st_blk = (None, 2, heads, dk, dk)
    s0_spec = pl.BlockSpec(st_blk, lambda i: (jnp.clip(seq_of(i) - lay.batch_p, 0, lay.batch_s - 1), 0, 0, 0, 0))
    sfin_spec = pl.BlockSpec(st_blk, lambda i: (seq_of(i), 0, 0, 0, 0))
    return pl.pallas_call(
        functools.partial(_hgrn_scan_kernel, lay=lay, tb=tb, heads=heads),
        out_shape=[jax.ShapeDtypeStruct((lay.nt, d), BF16), jax.ShapeDtypeStruct((lay.nt, d), BF16),
                   jax.ShapeDtypeStruct((nseq, 2, heads, dk, dk), F32)],
        grid=(lay.nt // tb,),
        in_specs=[fwd, fwd, fwd, cvf, bwd, bwd, bwd, cvb, s0_spec],
        out_specs=[fwd, bwd, sfin_spec],
        scratch_shapes=[pltpu.VMEM((2, heads, dk, dk), F32)],
        compiler_params=_cparams(("arbitrary",)),
        name="hgrn_scan",
    )(qf, kf, v, cv, qb, kb, v, cv, s0)


def _hgrn_out_kernel(x_ref, of_ref, ob_ref, sg_ref, mod_ref, ng_ref, wout_ref, o_ref, *, lay, heads):
    d = lay.d
    dk = d // heads
    _, _, gate = _mod_parts(mod_ref, d, first=True)
    o = of_ref[...].astype(F32) + ob_ref[...].astype(F32)
    parts = []
    for hd in range(heads):
        oh = o[:, hd * dk:(hd + 1) * dk]
        ms = jnp.mean(oh * oh, axis=-1, keepdims=True)
        parts.append(oh * lax.rsqrt(ms + RMS_EPS))
    on = jnp.concatenate(parts, axis=-1) * ng_ref[...]
    y = _dot((on * sg_ref[...].astype(F32)).astype(BF16), wout_ref[...])
    o_ref[...] = x_ref[...] + gate * y


def _hgrn_out(lay, x, o_f, o_b, sg, mods, layer, norm_g, j, w_out, heads):
    tb = TB
    d = lay.d
    blk = pl.BlockSpec((tb, d), lambda i: (i, 0))
    return pl.pallas_call(
        functools.partial(_hgrn_out_kernel, lay=lay, heads=heads),
        out_shape=jax.ShapeDtypeStruct((lay.nt, d), F32),
        grid=(lay.nt // tb,),
        in_specs=[blk, blk, blk, blk, _mod_spec(lay, layer, tb), _const_spec((1, d)), _layer_spec(w_out, j)],
        out_specs=blk,
        compiler_params=_cparams(("parallel",)),
        name="hgrn_out",
    )(x, o_f, o_b, sg, mods, norm_g, w_out)


def kernel(x_prompt, x_sample, state_rec, c, c_ctx, ada_w, ada_b, norm_g, final_g, conv_w_in, conv_w_dw,
           conv_w_out, pool_w, pool_scale, sgu_w_in, sgu_norm_g, sgu_w_s, sgu_b_s, sgu_w_out, hgrn_w_in,
           hgrn_lb, hgrn_norm_g, hgrn_w_out, ffn_w_up, ffn_w_dw, ffn_w_down):
    bp, lp, d = x_prompt.shape
    bs, ls, _ = x_sample.shape
    depth = ada_w.shape[0]
    heads = state_rec.shape[3]
    lay = _Layout(bp, lp, bs, ls, d)
    lay.check_block(TB)
    assert c.shape[0] == bs and 1 + bs <= 8

    cvec = jnp.concatenate([c_ctx[None], c, jnp.zeros((8 - 1 - bs, d), F32)], axis=0)
    mods = _ada_params(cvec, ada_w, ada_b).reshape(depth * 8, 1, 6 * d)

    assert N_MIXERS >= 1 and depth >= 1
    raw = (x_prompt.reshape(bp * lp, d), x_sample.reshape(bs * ls, d), _pos_table(ls, d))
    x = None

    wb = {k: v.astype(BF16) for k, v in dict(
        conv_in=conv_w_in, conv_out=conv_w_out, pool=pool_w, sgu_in=sgu_w_in, sgu_s=sgu_w_s, sgu_out=sgu_w_out,
        hgrn_in=hgrn_w_in, hgrn_out=hgrn_w_out, ffn_up=ffn_w_up, ffn_down=ffn_w_down).items()}

    new_states = []
    for i in range(depth):
        kind, j = i % N_MIXERS, i // N_MIXERS
        g1 = norm_g[i, 0].reshape(1, d)
        g2 = norm_g[i, 1].reshape(1, d)
        if kind == 0:
            x = _sconv_layer(lay, x, mods, i, g1, j, wb["conv_in"], conv_w_dw, wb["conv_out"],
                             raw=raw if i == 0 else None)
        elif kind == 1:
            x = _pool_layer(lay, x, mods, i, g1, j, wb["pool"], pool_scale[j].reshape(1, d))
        elif kind == 2:
            x = _sgu_layer(lay, x, mods, i, g1, j, wb["sgu_in"], sgu_norm_g[j].reshape(1, d),
                           wb["sgu_s"], sgu_b_s[j].T, wb["sgu_out"])
        else:
            qf, kf, qb, kb, v, sg, cv = _hgrn_proj(lay, x, mods, i, g1, j, wb["hgrn_in"],
                                                   hgrn_lb.reshape(depth, 2 * d))
            o_f, o_b, sfin = _hgrn_scan(lay, qf, kf, qb, kb, v, cv, state_rec[:, j], heads)
            new_states.append(sfin[:bp])
            x = _hgrn_out(lay, x, o_f, o_b, sg, mods, i, hgrn_norm_g[j].reshape(1, d), j, wb["hgrn_out"], heads)
        if i + 1 < depth:
            x = _ffn(lay, x, mods, i, g2, wb["ffn_up"], ffn_w_dw, wb["ffn_down"])
        else:
            y_prompt, y_sample = _ffn(lay, x, mods, i, g2, wb["ffn_up"], ffn_w_dw, wb["ffn_down"],
                                      final_g.reshape(1, d))
    y_prompt = y_prompt.reshape(bp, lp, d)
    y_sample = y_sample.reshape(bs, ls, d)
    new_state_rec = jnp.stack(new_states, axis=1)
    return (y_prompt, y_sample, new_state_rec)
```

```python
import functools
import math

import jax
import jax.numpy as jnp
from jax import lax
from jax.experimental import pallas as pl
from jax.experimental.pallas import tpu as pltpu

GRID_W = 64
N_MIXERS = 4
RMS_EPS = 1e-6
POOL_WINDOWS = (2, 4, 8, 16)
SGU_CHUNK = 128
POS_BASE = 10000.0

HALO = 8
TB = 512
FFN_TB = 256
FFN_FC = 256
HGRN_C = 64
HGRN_TB = 256
EXP_CLAMP = 80.0
F32_TINY = 1e-37
VMEM_LIMIT = 60000 * 1024

BF16 = jnp.bfloat16
F32 = jnp.float32


def _cparams(sem):
    return pltpu.CompilerParams(dimension_semantics=sem, vmem_limit_bytes=VMEM_LIMIT)


def _dot(a, b):
    return jnp.dot(a, b, preferred_element_type=F32)


def _const_spec(shape):
    nd = len(shape)
    return pl.BlockSpec(shape, lambda i: (0,) * nd)


def _layer_spec(stacked, idx):
    rest = stacked.shape[1:]
    return pl.BlockSpec((None,) + rest, lambda i: (idx,) + (0,) * len(rest))


class _Layout:
    def __init__(self, batch_p, seq_p, batch_s, seq_s, d):
        self.lp, self.ls, self.d = seq_p, seq_s, d
        self.np_rows = batch_p * seq_p
        self.ns_rows = batch_s * seq_s
        self.nt = self.np_rows + self.ns_rows
        self.batch_p, self.batch_s = batch_p, batch_s

    def check_block(self, tb):
        assert self.np_rows % tb == 0 and self.ns_rows % tb == 0
        for L in (self.lp, self.ls):
            assert L % tb == 0 or tb % L == 0

    def cond_row(self, r0):
        return jnp.where(r0 < self.np_rows, 0, 1 + (jnp.maximum(r0 - self.np_rows, 0)) // self.ls)


def _pos_in_seq(lay, r0, tb):
    k = lax.broadcasted_iota(jnp.int32, (tb, 1), 0)

    def pos_for(start, L):
        base = lax.rem(jnp.maximum(r0 - start, 0), L)
        p = base + k
        for j in range(1, (tb + L - 1) // L + 1):
            p = p - jnp.where(base + k >= j * L, L, 0)
        return p

    is_p = r0 < lay.np_rows
    pos = jnp.where(is_p, pos_for(0, lay.lp), pos_for(lay.np_rows, lay.ls))
    L = jnp.where(is_p, lay.lp, lay.ls)
    return pos, L


def _rms_mod(x, g, shift, scale):
    ms = jnp.mean(x * x, axis=-1, keepdims=True)
    return x * lax.rsqrt(ms + RMS_EPS) * (g * (1.0 + scale)) + shift


def _mod_parts(mod_ref, d, first):
    o = 0 if first else 3 * d
    return (mod_ref[:, o:o + d], mod_ref[:, o + d:o + 2 * d], mod_ref[:, o + 2 * d:o + 3 * d])


def _halo_specs(lay, tb):
    nb8 = lay.nt // HALO
    r = tb // HALO
    return [
        pl.BlockSpec((tb, lay.d), lambda i: (i, 0)),
        pl.BlockSpec((HALO, lay.d), lambda i: (jnp.maximum(i * r - 1, 0), 0)),
        pl.BlockSpec((HALO, lay.d), lambda i: (jnp.minimum((i + 1) * r, nb8 - 1), 0)),
    ]


def _mod_spec(lay, layer, tb):
    return pl.BlockSpec((None, 1, 6 * lay.d), lambda i: (layer * 8 + lay.cond_row(i * tb), 0, 0))


def _edge_groups(lay, tb):
    step = math.gcd(math.gcd(lay.lp, lay.ls), tb)
    assert step % HALO == 0
    firsts = list(range(0, tb, step))
    lasts = [(k - HALO) % tb for k in firsts]
    return firsts, lasts


def _zero_rows(a, mask, groups):
    pieces, cur = [], 0
    for r in sorted(groups):
        if r > cur:
            pieces.append(a[cur:r])
        pieces.append(jnp.where(mask[r:r + HALO], 0.0, a[r:r + HALO]))
        cur = r + HALO
    if cur < a.shape[0]:
        pieces.append(a[cur:])
    return jnp.concatenate(pieces, axis=0)


def _dwconv3(a_ref, w, first, last, tb, edges):
    prev = a_ref[HALO - 1:HALO - 1 + tb, :]
    mid = a_ref[HALO:HALO + tb, :]
    nxt = a_ref[HALO + 1:HALO + 1 + tb, :]
    prev = _zero_rows(prev, first, edges[0])
    nxt = _zero_rows(nxt, last, edges[1])
    return prev * w[0:1] + mid * w[1:2] + nxt * w[2:3]


def _pos_kernel(o_ref, *, tb, d):
    i = pl.program_id(0)
    q = d // 4
    nr = tb // GRID_W
    j = lax.broadcasted_iota(jnp.int32, (1, q), 1).astype(F32)
    freq = jnp.exp(-math.log(POS_BASE) * j / q)
    rr = (i * nr + lax.broadcasted_iota(jnp.int32, (nr, 1), 0)).astype(F32)
    cc = lax.broadcasted_iota(jnp.int32, (GRID_W, 1), 0).astype(F32)
    ar = rr * freq
    ac = cc * freq
    row_part = jnp.concatenate([jnp.sin(ar), jnp.cos(ar)], axis=1)
    col_part = jnp.concatenate([jnp.sin(ac), jnp.cos(ac)], axis=1)
    for r in range(nr):
        rows = slice(r * GRID_W, (r + 1) * GRID_W)
        o_ref[rows, 0:2 * q] = jnp.broadcast_to(row_part[r:r + 1], (GRID_W, 2 * q))
        o_ref[rows, 2 * q:4 * q] = col_part


def _pos_table(n_tokens, d):
    tb = 512
    assert tb % GRID_W == 0 and n_tokens % tb == 0
    return pl.pallas_call(
        functools.partial(_pos_kernel, tb=tb, d=d),
        out_shape=jax.ShapeDtypeStruct((n_tokens, d), F32),
        grid=(n_tokens // tb,),
        out_specs=pl.BlockSpec((tb, d), lambda i: (i, 0)),
        compiler_params=_cparams(("parallel",)),
        name="pos_table",
    )()


def _ada_kernel(c_ref, w_ref, b_ref, o_ref):
    c = c_ref[...]
    s = (c * jax.nn.sigmoid(c)).astype(BF16)
    o_ref[...] = _dot(s, w_ref[...].astype(BF16)) + b_ref[...]


def _ada_params(cvec, ada_w, ada_b):
    depth, d, n = ada_w.shape
    tn = 1536
    assert n % tn == 0
    return pl.pallas_call(
        _ada_kernel,
        out_shape=jax.ShapeDtypeStruct((depth, 8, n), F32),
        grid=(depth, n // tn),
        in_specs=[
            pl.BlockSpec((8, d), lambda l, j: (0, 0)),
            pl.BlockSpec((None, d, tn), lambda l, j: (l, 0, j)),
            pl.BlockSpec((None, 1, tn), lambda l, j: (l, 0, j)),
        ],
        out_specs=pl.BlockSpec((None, 8, tn), lambda l, j: (l, 0, j)),
        compiler_params=_cparams(("parallel", "parallel")),
        name="ada_params",
    )(cvec, ada_w, ada_b.reshape(depth, 1, n))


def _edge_points(lay, r0, tb):
    is_p = r0 < lay.np_rows
    base = jnp.where(is_p, lax.rem(r0, lay.lp), lax.rem(jnp.maximum(r0 - lay.np_rows, 0), lay.ls))
    L = jnp.where(is_p, lay.lp, lay.ls)
    step = math.gcd(math.gcd(lay.lp, lay.ls), tb)
    firsts = [(k, lax.rem(base + k, L) == 0) for k in range(0, tb, step)]
    lasts = [((k - 1) % tb, lax.rem(base + ((k - 1) % tb) + 1, L) == 0) for k in range(0, tb, step)]
    return firsts, lasts


def _zero_points(a, points, s_rows):
    sub = lax.broadcasted_iota(jnp.int32, (HALO, 1), 0)
    pieces, cur = [], 0
    for k, flag in sorted(points, key=lambda p: (p[0] + HALO) % s_rows):
        s, j = divmod(k + HALO, s_rows)
        r = HALO * j
        assert r >= cur
        if r > cur:
            pieces.append(a[cur:r])
        pieces.append(jnp.where(jnp.logical_and(sub == s, flag), 0.0, a[r:r + HALO]))
        cur = r + HALO
    if cur < a.shape[0]:
        pieces.append(a[cur:])
    return jnp.concatenate(pieces, axis=0)


def _dwconv3_strided(u_ref, w, firsts, lasts):
    n = u_ref.shape[0]
    s_rows = n // HALO
    mid = u_ref[...]
    prev = jnp.concatenate([pltpu.roll(u_ref[n - HALO:n, :], 1, axis=0), u_ref[0:n - HALO, :]], axis=0)
    nxt = jnp.concatenate([u_ref[HALO:n, :], pltpu.roll(u_ref[0:HALO, :], HALO - 1, axis=0)], axis=0)
    prev = _zero_points(prev, firsts, s_rows)
    nxt = _zero_points(nxt, lasts, s_rows)
    return prev * w[0:1] + mid * w[1:2] + nxt * w[2:3]


def _ffn_kernel(*refs, lay, tb, nf, final):
    if final:
        (x_ref, xp_ref, xn_ref, mod_ref, g_ref, wup_ref, wdw_ref, wdn_ref, fg_ref,
         yp_ref, ys_ref, slab_ref, u_ref) = refs
    else:
        x_ref, xp_ref, xn_ref, mod_ref, g_ref, wup_ref, wdw_ref, wdn_ref, o_ref, slab_ref, u_ref = refs
    d = lay.d
    fc = FFN_FC
    n = tb + 2 * HALO
    s_rows = n // HALO
    nl = d // 128
    i = pl.program_id(0)
    firsts, lasts = _edge_points(lay, i * tb, tb)
    shift, scale, gate = _mod_parts(mod_ref, d, first=False)

    for l in range(nl):
        cols = slice(128 * l, 128 * (l + 1))
        slab_ref[l, 0:HALO, :] = xp_ref[:, cols]
        slab_ref[l, HALO:HALO + tb, :] = x_ref[:, cols]
        slab_ref[l, HALO + tb:n, :] = xn_ref[:, cols]
    xs = jnp.concatenate(
        [jnp.concatenate([slab_ref[l, pl.ds(j, HALO, stride=s_rows), :] for l in range(nl)], axis=1)
         for j in range(s_rows)], axis=0)
    h = _rms_mod(xs, g_ref[...], shift, scale).astype(BF16)

    def up(c):
        for ab in range(2):
            u_ref[c % 2, ab] = _dot(h, wup_ref[:, ab * nf * fc + c * fc:ab * nf * fc + (c + 1) * fc])

    up(0)
    acc = None
    for c in range(nf):
        if c + 1 < nf:
            up(c + 1)
        ca = _dwconv3_strided(u_ref.at[c % 2, 0], wdw_ref[:, c * fc:(c + 1) * fc], firsts, lasts)
        cb = _dwconv3_strided(u_ref.at[c % 2, 1], wdw_ref[:, (nf + c) * fc:(nf + c + 1) * fc], firsts, lasts)
        act = (ca * jax.nn.sigmoid(ca) * cb).astype(BF16)
        part = _dot(act, wdn_ref[c * fc:(c + 1) * fc, :])
        acc = part if acc is None else acc + part

    for l in range(nl):
        for j in range(s_rows):
            slab_ref[l, pl.ds(j, HALO, stride=s_rows), :] = acc[HALO * j:HALO * (j + 1), 128 * l:128 * (l + 1)]
    y = jnp.concatenate([slab_ref[l, HALO:HALO + tb, :] for l in range(nl)], axis=1)
    out = x_ref[...] + gate * y
    if not final:
        o_ref[...] = out
    else:
        ms = jnp.mean(out * out, axis=-1, keepdims=True)
        out = out * lax.rsqrt(ms + RMS_EPS) * fg_ref[...]
        is_p = i * tb < lay.np_rows

        @pl.when(is_p)
        def _():
            yp_ref[...] = out

        @pl.when(jnp.logical_not(is_p))
        def _():
            ys_ref[...] = out


def _ffn(lay, x, mods, layer, g, wup, wdw, wdn, final_g=None):
    tb = FFN_TB
    d = lay.d
    assert wdn.shape[1] % FFN_FC == 0
    nf = wdn.shape[1] // FFN_FC
    n = tb + 2 * HALO
    lay.check_block(tb)
    assert n % HALO == 0 and d % 128 == 0
    final = final_g is not None
    blk = pl.BlockSpec((tb, d), lambda i: (i, 0))
    if final:
        npb = lay.np_rows // tb
        out_shape = [jax.ShapeDtypeStruct((lay.np_rows, d), F32), jax.ShapeDtypeStruct((lay.ns_rows, d), F32)]
        out_specs = [pl.BlockSpec((tb, d), lambda i: (jnp.minimum(i, npb - 1), 0)),
                     pl.BlockSpec((tb, d), lambda i: (jnp.maximum(i - npb, 0), 0))]
    else:
        out_shape = jax.ShapeDtypeStruct((lay.nt, d), F32)
        out_specs = blk
    return pl.pallas_call(
        functools.partial(_ffn_kernel, lay=lay, tb=tb, nf=nf, final=final),
        out_shape=out_shape,
        grid=(lay.nt // tb,),
        in_specs=_halo_specs(lay, tb) + [
            _mod_spec(lay, layer, tb),
            _const_spec((1, d)),
            _layer_spec(wup, layer),
            _layer_spec(wdw, layer),
            _layer_spec(wdn, layer),
        ] + ([_const_spec((1, d))] if final else []),
        out_specs=out_specs,
        scratch_shapes=[pltpu.VMEM((d // 128, n, 128), F32),
                        pltpu.VMEM((2, 2, n, FFN_FC), F32)],
        compiler_params=_cparams(("arbitrary",)),
        name="conv_ffn",
    )(x, x, x, mods, g, wup, wdw, wdn, *([final_g] if final else []))


def _sconv_kernel(*refs, lay, tb, embed):
    if embed:
        (xp_ref, xpp_ref, xpn_ref, xs_ref, xsp_ref, xsn_ref, ps_ref, psp_ref, psn_ref,
         mod_ref, g_ref, win_ref, wdw_ref, wout_ref, o_ref, xh_ref, p_ref) = refs
    else:
        x_ref, xp_ref, xn_ref, mod_ref, g_ref, win_ref, wdw_ref, wout_ref, o_ref, xh_ref, p_ref = refs
    d = lay.d
    n = tb + 2 * HALO
    i = pl.program_id(0)
    pos, L = _pos_in_seq(lay, i * tb, tb)
    first, last = pos == 0, pos == L - 1
    shift, scale, gate = _mod_parts(mod_ref, d, first=True)
    if embed:
        is_p = i * tb < lay.np_rows

        @pl.when(is_p)
        def _():
            xh_ref[0:HALO, :] = xpp_ref[...]
            xh_ref[HALO:HALO + tb, :] = xp_ref[...]
            xh_ref[HALO + tb:n, :] = xpn_ref[...]

        @pl.when(jnp.logical_not(is_p))
        def _():
            xh_ref[0:HALO, :] = xsp_ref[...] + psp_ref[...]
            xh_ref[HALO:HALO + tb, :] = xs_ref[...] + ps_ref[...]
            xh_ref[HALO + tb:n, :] = xsn_ref[...] + psn_ref[...]
    else:
        xh_ref[0:HALO, :] = xp_ref[...]
        xh_ref[HALO:HALO + tb, :] = x_ref[...]
        xh_ref[HALO + tb:n, :] = xn_ref[...]
    h = _rms_mod(xh_ref[...], g_ref[...], shift, scale).astype(BF16)
    bg = _dot(h, win_ref[:, 0:d])[HALO:HALO + tb]
    p_ref[...] = _dot(h, win_ref[:, d:2 * d]) * _dot(h, win_ref[:, 2 * d:3 * d])
    conv = _dwconv3(p_ref, wdw_ref[...], first, last, tb, _edge_groups(lay, tb))
    y = _dot((bg * conv).astype(BF16), wout_ref[...])
    o_ref[...] = xh_ref[HALO:HALO + tb, :] + gate * y


def _stream_halo_specs(rows, d, tb, blk_of):
    r = tb // HALO
    last8 = rows // HALO - 1
    return [
        pl.BlockSpec((tb, d), lambda i: (blk_of(i), 0)),
        pl.BlockSpec((HALO, d), lambda i: (jnp.clip(blk_of(i) * r - 1, 0, last8), 0)),
        pl.BlockSpec((HALO, d), lambda i: (jnp.clip((blk_of(i) + 1) * r, 0, last8), 0)),
    ]


def _sconv_layer(lay, x, mods, layer, g, j, w_in, w_dw, w_out, raw=None):
    tb = TB
    d = lay.d
    embed = raw is not None
    if embed:
        npb = lay.np_rows // tb
        psb = lay.ls // tb
        xspecs = (_stream_halo_specs(lay.np_rows, d, tb, lambda i: jnp.minimum(i, npb - 1))
                  + _stream_halo_specs(lay.ns_rows, d, tb, lambda i: jnp.maximum(i - npb, 0))
                  + _stream_halo_specs(lay.ls, d, tb, lambda i: lax.rem(jnp.maximum(i - npb, 0), psb)))
        xargs = (raw[0],) * 3 + (raw[1],) * 3 + (raw[2],) * 3
    else:
        xspecs = _halo_specs(lay, tb)
        xargs = (x, x, x)
    return pl.pallas_call(
        functools.partial(_sconv_kernel, lay=lay, tb=tb, embed=embed),
        out_shape=jax.ShapeDtypeStruct((lay.nt, d), F32),
        grid=(lay.nt // tb,),
        in_specs=xspecs + [
            _mod_spec(lay, layer, tb),
            _const_spec((1, d)),
            _layer_spec(w_in, j),
            _layer_spec(w_dw, j),
            _layer_spec(w_out, j),
        ],
        out_specs=pl.BlockSpec((tb, d), lambda i: (i, 0)),
        scratch_shapes=[pltpu.VMEM((tb + 2 * HALO, d), F32), pltpu.VMEM((tb + 2 * HALO, d), F32)],
        compiler_params=_cparams(("arbitrary",)),
        name="sconv_mixer",
    )(*xargs, mods, g, w_in, w_dw, w_out)


def _pool_kernel(x_ref, xp_ref, xn_ref, mod_ref, g_ref, w_ref, sc_ref, o_ref, h_ref, *, lay, tb):
    d = lay.d
    ng = len(POOL_WINDOWS)
    pg = d // ng
    i = pl.program_id(0)
    pos, L = _pos_in_seq(lay, i * tb, tb)
    shift, scale, gate = _mod_parts(mod_ref, d, first=True)
    xh = jnp.concatenate([xp_ref[...], x_ref[...], xn_ref[...]], axis=0)
    h_ref[...] = _rms_mod(xh, g_ref[...], shift, scale)
    n = tb + 2 * HALO
    edge_groups = sorted(set(_edge_groups(lay, tb)[0]) | set(_edge_groups(lay, tb)[1]))
    outs = []
    for gi, w in enumerate(POOL_WINDOWS):
        assert w // 2 <= HALO and w & (w - 1) == 0
        cols = slice(gi * pg, (gi + 1) * pg)
        centre = h_ref[HALO:HALO + tb, cols]
        p = h_ref[:, cols]
        m = 1
        while m < w:
            p = p + pltpu.roll(p, m, axis=0)
            m *= 2
        ahead = w // 2 - 1
        if ahead:
            p = pltpu.roll(p, n - ahead, axis=0)
        mean = p[HALO:HALO + tb] * (1.0 / w)
        pieces, cur = [], 0
        for r in edge_groups:
            if r > cur:
                pieces.append(mean[cur:r])
            pr = pos[r:r + HALO]
            s = jnp.zeros((HALO, pg), F32)
            cnt = jnp.zeros((HALO, 1), F32)
            for k in range(-(w // 2), w // 2):
                inside = jnp.logical_and(pr + k >= 0, pr + k <= L - 1)
                s = s + jnp.where(inside, h_ref[HALO + r + k:HALO + r + k + HALO, cols], 0.0)
                cnt = cnt + jnp.where(inside, 1.0, 0.0)
            pieces.append(s / cnt)
            cur = r + HALO
        if cur < tb:
            pieces.append(mean[cur:])
        mean = jnp.concatenate(pieces, axis=0)
        pooled = (mean - centre).astype(BF16)
        outs.append(_dot(pooled, w_ref[gi]))
    y = jnp.concatenate(outs, axis=-1) * sc_ref[...]
    o_ref[...] = x_ref[...] + gate * y


def _pool_layer(lay, x, mods, layer, g, j, w, sc):
    tb = TB
    d = lay.d
    return pl.pallas_call(
        functools.partial(_pool_kernel, lay=lay, tb=tb),
        out_shape=jax.ShapeDtypeStruct((lay.nt, d), F32),
        grid=(lay.nt // tb,),
        in_specs=_halo_specs(lay, tb) + [
            _mod_spec(lay, layer, tb),
            _const_spec((1, d)),
            _layer_spec(w, j),
            _const_spec((1, d)),
        ],
        out_specs=pl.BlockSpec((tb, d), lambda i: (i, 0)),
        scratch_shapes=[pltpu.VMEM((tb + 2 * HALO, d), F32)],
        compiler_params=_cparams(("parallel",)),
        name="pool_mixer",
    )(x, x, x, mods, g, w, sc)


def _sgu_kernel(x_ref, mod_ref, g_ref, win_ref, ng_ref, ws_ref, bs_ref, wout_ref, o_ref, s_ref, *, lay, tb):
    d = lay.d
    groups = ws_ref.shape[0]
    gd = d // groups
    shift, scale, gate = _mod_parts(mod_ref, d, first=True)
    x = x_ref[...]
    h = _rms_mod(x, g_ref[...], shift, scale).astype(BF16)
    v = jax.nn.gelu(_dot(h, win_ref[:, d:2 * d]), approximate=True)
    u = jax.nn.gelu(_dot(h, win_ref[:, 0:d]), approximate=True)
    ms = jnp.mean(v * v, axis=-1, keepdims=True)
    vb = (v * lax.rsqrt(ms + RMS_EPS) * ng_ref[...]).astype(BF16)
    for n in range(tb // SGU_CHUNK):
        for gi in range(groups):
            rows = slice(n * SGU_CHUNK, (n + 1) * SGU_CHUNK)
            cols = slice(gi * gd, (gi + 1) * gd)
            s_ref[rows, cols] = _dot(ws_ref[gi], vb[rows, cols]) + bs_ref[:, gi:gi + 1]
    y = _dot((u * s_ref[...]).astype(BF16), wout_ref[...])
    o_ref[...] = x + gate * y


def _sgu_layer(lay, x, mods, layer, g, j, w_in, norm_g, w_s, b_st, w_out):
    tb = TB
    d = lay.d
    assert tb % SGU_CHUNK == 0 and lay.lp % SGU_CHUNK == 0 and lay.ls % SGU_CHUNK == 0
    return pl.pallas_call(
        functools.partial(_sgu_kernel, lay=lay, tb=tb),
        out_shape=jax.ShapeDtypeStruct((lay.nt, d), F32),
        grid=(lay.nt // tb,),
        in_specs=[
            pl.BlockSpec((tb, d), lambda i: (i, 0)),
            _mod_spec(lay, layer, tb),
            _const_spec((1, d)),
            _layer_spec(w_in, j),
            _const_spec((1, d)),
            _layer_spec(w_s, j),
            _const_spec(b_st.shape),
            _layer_spec(w_out, j),
        ],
        out_specs=pl.BlockSpec((tb, d), lambda i: (i, 0)),
        scratch_shapes=[pltpu.VMEM((tb, d), F32)],
        compiler_params=_cparams(("parallel",)),
        name="sgu_mixer",
    )(x, mods, g, w_in, norm_g, w_s, b_st, w_out)


def _split2(a):
    hi = a.astype(BF16)
    lo = (a - hi.astype(F32)).astype(BF16)
    return hi, lo


def _hgrn_gates(z, lb, one_m_lb):
    ez = jnp.exp(-jnp.abs(z))
    r = 1.0 / (1.0 + ez)
    er = ez * r
    pos = z >= 0.0
    f = lb + one_m_lb * jnp.where(pos, r, er)
    logf = jnp.log(jnp.maximum(f, F32_TINY))
    return logf, one_m_lb * jnp.where(pos, er, r)


def _hgrn_proj_kernel(x_ref, mod_ref, g_ref, win_ref, lb_ref,
                      qf_ref, kf_ref, qb_ref, kb_ref, v_ref, sg_ref, cv_ref, *, lay, tb, layer):
    d = lay.d
    c = HGRN_C
    shift, scale, _ = _mod_parts(mod_ref, d, first=True)
    h = _rms_mod(x_ref[...], g_ref[...], shift, scale).astype(BF16)

    lbp = lb_ref[...]
    e = jnp.exp(lbp - jnp.max(lbp, axis=0, keepdims=True))
    p = e / jnp.sum(e, axis=0, keepdims=True)
    lb = jnp.zeros((1, 2 * d), F32)
    for j in range(1, layer + 1):
        lb = lb + p[j:j + 1]
    one_m_lb = 1.0 - lb

    z = [_dot(h, win_ref[:, (1 + di) * d:(2 + di) * d]) for di in range(2)]
    q = _dot(h, win_ref[:, 0:d])
    gates = [_hgrn_gates(z[di], lb[:, di * d:(di + 1) * d], one_m_lb[:, di * d:(di + 1) * d]) for di in range(2)]

    rt = lax.broadcasted_iota(jnp.int32, (c, 2 * c), 0)
    cs = lax.broadcasted_iota(jnp.int32, (c, 2 * c), 1)
    cs = jnp.where(cs >= c, cs - c, cs)
    tri_f = (cs <= rt).astype(BF16)
    tri_b = (cs >= rt).astype(BF16)
    mid = c // 2

    def cumsum(di, tri):
        logf = gates[di][0]
        return [_dot(tri, jnp.concatenate(_split2(logf[n * c:(n + 1) * c]), axis=0)) for n in range(tb // c)]

    cum = [cumsum(0, tri_f)]
    vv = _dot(h, win_ref[:, 3 * d:4 * d])
    cum.append(cumsum(1, tri_b))
    gg = _dot(h, win_ref[:, 4 * d:5 * d])

    for di, (q_ref, k_ref) in enumerate(((qf_ref, kf_ref), (qb_ref, kb_ref))):
        kk = gates[di][1]
        for n in range(tb // c):
            rows = slice(n * c, (n + 1) * c)
            b = cum[di][n]
            if di == 0:
                btot = b[c - 1:c]
                bref = b[mid - 1:mid]
            else:
                btot = b[0:1]
                bref = b[mid:mid + 1]
            grow = jnp.exp(jnp.clip(b - bref, -EXP_CLAMP, EXP_CLAMP))
            q_ref[rows, :] = (q[rows] * grow).astype(BF16)
            k_ref[rows, :] = (kk[rows] * pl.reciprocal(grow, approx=True)).astype(BF16)
            cv_ref[n, 3 * di:3 * di + 1, :] = jnp.exp(bref)
            cv_ref[n, 3 * di + 1:3 * di + 2, :] = jnp.exp(btot - bref)
            cv_ref[n, 3 * di + 2:3 * di + 3, :] = jnp.exp(btot)
    v_ref[...] = vv.astype(BF16)
    sg_ref[...] = (gg * jax.nn.sigmoid(gg)).astype(BF16)
    for n in range(tb // c):
        cv_ref[n, 6:8, :] = jnp.zeros((2, d), F32)


def _hgrn_proj(lay, x, mods, layer, g, j, w_in, lb2):
    tb = TB
    d = lay.d
    c = HGRN_C
    blk = pl.BlockSpec((tb, d), lambda i: (i, 0))
    act = jax.ShapeDtypeStruct((lay.nt, d), BF16)
    return pl.pallas_call(
        functools.partial(_hgrn_proj_kernel, lay=lay, tb=tb, layer=layer),
        out_shape=[act, act, act, act, act, act, jax.ShapeDtypeStruct((lay.nt // c, 8, d), F32)],
        grid=(lay.nt // tb,),
        in_specs=[blk, _mod_spec(lay, layer, tb), _const_spec((1, d)), _layer_spec(w_in, j),
                  _const_spec(lb2.shape)],
        out_specs=[blk, blk, blk, blk, blk, blk, pl.BlockSpec((tb // c, 8, d), lambda i: (i, 0, 0))],
        compiler_params=_cparams(("parallel",)),
        name="hgrn_proj",
    )(x, mods, g, w_in, lb2)


def _hgrn_scan_kernel(qf_ref, kf_ref, vf_ref, cvf_ref, qb_ref, kb_ref, vb_ref, cvb_ref, s0_ref,
                      of_ref, ob_ref, sfin_ref, st_ref, *, lay, tb, heads):
    d = lay.d
    c = HGRN_C
    dk = d // heads
    i = pl.program_id(0)
    r0 = i * tb
    is_p = r0 < lay.np_rows
    base = jnp.where(is_p, lax.rem(r0, lay.lp), lax.rem(jnp.maximum(r0 - lay.np_rows, 0), lay.ls))
    L = jnp.where(is_p, lay.lp, lay.ls)

    @pl.when(jnp.logical_and(base == 0, is_p))
    def _():
        st_ref[...] = jnp.zeros_like(st_ref)

    @pl.when(jnp.logical_and(base == 0, jnp.logical_not(is_p)))
    def _():
        for di in range(2):
            for hd in range(heads):
                st_ref[di, hd] = s0_ref[di, hd].T

    rt = lax.broadcasted_iota(jnp.int32, (c, c), 0)
    cs = lax.broadcasted_iota(jnp.int32, (c, c), 1)
    keep = (cs <= rt, cs >= rt)
    nchunk = tb // c
    dirs = ((qf_ref, kf_ref, vf_ref, cvf_ref, of_ref), (qb_ref, kb_ref, vb_ref, cvb_ref, ob_ref))
    nt_dims = (((1,), (1,)), ((), ()))
    tn_dims = (((0,), (0,)), ((), ()))
    st = [[st_ref[di, hd] for hd in range(heads)] for di in range(2)]
    for n in range(nchunk):
        part = {}
        for di, (q_ref, k_ref, v_ref, cv_ref, _) in enumerate(dirs):
            nn = n if di == 0 else nchunk - 1 - n
            rows = slice(nn * c, (nn + 1) * c)
            for hd in range(heads):
                cols = slice(hd * dk, (hd + 1) * dk)
                qs, ks, vv = q_ref[rows, cols], k_ref[rows, cols], v_ref[rows, cols]
                ku = (ks.astype(F32) * cv_ref[nn, 3 * di + 1:3 * di + 2, cols]).astype(BF16)
                sc = lax.dot_general(qs, ks, nt_dims, preferred_element_type=F32)
                upd = lax.dot_general(vv, ku, tn_dims, preferred_element_type=F32)
                part[di, hd] = (qs, vv, sc, upd)
        for di, (_, _, _, cv_ref, o_ref) in enumerate(dirs):
            nn = n if di == 0 else nchunk - 1 - n
            rows = slice(nn * c, (nn + 1) * c)
            for hd in range(heads):
                cols = slice(hd * dk, (hd + 1) * dk)
                qs, vv, sc, upd = part[di, hd]
                scb = jnp.where(keep[di], sc, 0.0).astype(BF16)
                qi = (qs.astype(F32) * cv_ref[nn, 3 * di:3 * di + 1, cols]).astype(BF16)
                o_ref[rows, cols] = (_dot(scb, vv) + lax.dot_general(
                    qi, st[di][hd].astype(BF16), nt_dims, preferred_element_type=F32)).astype(BF16)
                st[di][hd] = st[di][hd] * cv_ref[nn, 3 * di + 2:3 * di + 3, cols] + upd
    for di in range(2):
        for hd in range(heads):
            st_ref[di, hd] = st[di][hd]

    @pl.when(jnp.logical_and(base + tb == L, is_p))
    def _():
        for di in range(2):
            for hd in range(heads):
                sfin_ref[di, hd] = st_ref[di, hd].T


def _hgrn_scan(lay, qf, kf, qb, kb, v, cv, s0, heads):
    tb = HGRN_TB
    d = lay.d
    c = HGRN_C
    dk = d // heads
    lay.check_block(tb)
    assert tb <= lay.lp and tb <= lay.ls
    npb = lay.np_rows // tb
    bps_p, bps_s = lay.lp // tb, lay.ls // tb

    def mirror(i):
        ip = (i // bps_p) * bps_p + (bps_p - 1 - lax.rem(i, bps_p))
        j = jnp.maximum(i - npb, 0)
        isx = npb + (j // bps_s) * bps_s + (bps_s - 1 - lax.rem(j, bps_s))
        return jnp.where(i < npb, ip, isx)

    def seq_of(i):
        return jnp.where(i < npb, i // bps_p, lay.batch_p + jnp.maximum(i - npb, 0) // bps_s)

    fwd = pl.BlockSpec((tb, d), lambda i: (i, 0))
    bwd = pl.BlockSpec((tb, d), lambda i: (mirror(i), 0))
    cvf = pl.BlockSpec((tb // c, 8, d), lambda i: (i, 0, 0))
    cvb = pl.BlockSpec((tb // c, 8, d), lambda i: (mirror(i), 0, 0))
    st_blk = (None, 2, heads, dk, dk)
    s0_spec = pl.BlockSpec(st_blk, lambda i: (jnp.clip(seq_of(i) - lay.batch_p, 0, lay.batch_s - 1), 0, 0, 0, 0))
    sfin_spec = pl.BlockSpec(st_blk, lambda i: (jnp.minimum(seq_of(i), lay.batch_p - 1), 0, 0, 0, 0))
    return pl.pallas_call(
        functools.partial(_hgrn_scan_kernel, lay=lay, tb=tb, heads=heads),
        out_shape=[jax.ShapeDtypeStruct((lay.nt, d), BF16), jax.ShapeDtypeStruct((lay.nt, d), BF16),
                   jax.ShapeDtypeStruct((lay.batch_p, 2, heads, dk, dk), F32)],
        grid=(lay.nt // tb,),
        in_specs=[fwd, fwd, fwd, cvf, bwd, bwd, bwd, cvb, s0_spec],
        out_specs=[fwd, bwd, sfin_spec],
        scratch_shapes=[pltpu.VMEM((2, heads, dk, dk), F32)],
        compiler_params=_cparams(("arbitrary",)),
        name="hgrn_scan",
    )(qf, kf, v, cv, qb, kb, v, cv, s0)


def _hgrn_out_kernel(x_ref, of_ref, ob_ref, sg_ref, mod_ref, ng_ref, wout_ref, o_ref, *, lay, heads):
    d = lay.d
    dk = d // heads
    _, _, gate = _mod_parts(mod_ref, d, first=True)
    o = of_ref[...].astype(F32) + ob_ref[...].astype(F32)
    parts = []
    for hd in range(heads):
        oh = o[:, hd * dk:(hd + 1) * dk]
        ms = jnp.mean(oh * oh, axis=-1, keepdims=True)
        parts.append(oh * lax.rsqrt(ms + RMS_EPS))
    on = jnp.concatenate(parts, axis=-1) * ng_ref[...]
    y = _dot((on * sg_ref[...].astype(F32)).astype(BF16), wout_ref[...])
    o_ref[...] = x_ref[...] + gate * y


def _hgrn_out(lay, x, o_f, o_b, sg, mods, layer, norm_g, j, w_out, heads):
    tb = TB
    d = lay.d
    blk = pl.BlockSpec((tb, d), lambda i: (i, 0))
    return pl.pallas_call(
        functools.partial(_hgrn_out_kernel, lay=lay, heads=heads),
        out_shape=jax.ShapeDtypeStruct((lay.nt, d), F32),
        grid=(lay.nt // tb,),
        in_specs=[blk, blk, blk, blk, _mod_spec(lay, layer, tb), _const_spec((1, d)), _layer_spec(w_out, j)],
        out_specs=blk,
        compiler_params=_cparams(("parallel",)),
        name="hgrn_out",
    )(x, o_f, o_b, sg, mods, norm_g, w_out)


def kernel(x_prompt, x_sample, state_rec, c, c_ctx, ada_w, ada_b, norm_g, final_g, conv_w_in, conv_w_dw,
           conv_w_out, pool_w, pool_scale, sgu_w_in, sgu_norm_g, sgu_w_s, sgu_b_s, sgu_w_out, hgrn_w_in,
           hgrn_lb, hgrn_norm_g, hgrn_w_out, ffn_w_up, ffn_w_dw, ffn_w_down):
    bp, lp, d = x_prompt.shape
    bs, ls, _ = x_sample.shape
    depth = ada_w.shape[0]
    heads = state_rec.shape[3]
    lay = _Layout(bp, lp, bs, ls, d)
    lay.check_block(TB)
    assert c.shape[0] == bs and 1 + bs <= 8

    cvec = jnp.concatenate([c_ctx[None], c, jnp.zeros((8 - 1 - bs, d), F32)], axis=0)
    mods = _ada_params(cvec, ada_w, ada_b).reshape(depth * 8, 1, 6 * d)

    assert N_MIXERS >= 1 and depth >= 1
    raw = (x_prompt.reshape(bp * lp, d), x_sample.reshape(bs * ls, d), _pos_table(ls, d))
    x = None

    wb = {k: v.astype(BF16) for k, v in dict(
        conv_in=conv_w_in, conv_out=conv_w_out, pool=pool_w, sgu_in=sgu_w_in, sgu_s=sgu_w_s, sgu_out=sgu_w_out,
        hgrn_in=hgrn_w_in, hgrn_out=hgrn_w_out, ffn_up=ffn_w_up, ffn_down=ffn_w_down).items()}

    new_states = []
    for i in range(depth):
        kind, j = i % N_MIXERS, i // N_MIXERS
        g1 = norm_g[i, 0].reshape(1, d)
        g2 = norm_g[i, 1].reshape(1, d)
        if kind == 0:
            x = _sconv_layer(lay, x, mods, i, g1, j, wb["conv_in"], conv_w_dw, wb["conv_out"],
                             raw=raw if i == 0 else None)
        elif kind == 1:
            x = _pool_layer(lay, x, mods, i, g1, j, wb["pool"], pool_scale[j].reshape(1, d))
        elif kind == 2:
            x = _sgu_layer(lay, x, mods, i, g1, j, wb["sgu_in"], sgu_norm_g[j].reshape(1, d),
                           wb["sgu_s"], sgu_b_s[j].T, wb["sgu_out"])
        else:
            qf, kf, qb, kb, v, sg, cv = _hgrn_proj(lay, x, mods, i, g1, j, wb["hgrn_in"],
                                                   hgrn_lb.reshape(depth, 2 * d))
            o_f, o_b, sfin = _hgrn_scan(lay, qf, kf, qb, kb, v, cv, state_rec[:, j], heads)
            new_states.append(sfin)
            x = _hgrn_out(lay, x, o_f, o_b, sg, mods, i, hgrn_norm_g[j].reshape(1, d), j, wb["hgrn_out"], heads)
        if i + 1 < depth:
            x = _ffn(lay, x, mods, i, g2, wb["ffn_up"], ffn_w_dw, wb["ffn_down"])
        else:
            y_prompt, y_sample = _ffn(lay, x, mods, i, g2, wb["ffn_up"], ffn_w_dw, wb["ffn_down"],
                                      final_g.reshape(1, d))
    y_prompt = y_prompt.reshape(bp, lp, d)
    y_sample = y_sample.reshape(bs, ls, d)
    new_state_rec = jnp.stack(new_states, axis=1)
    return (y_prompt, y_sample, new_state_rec)
```

```python
import functools
import math

import jax
import jax.numpy as jnp
from jax import lax
from jax.experimental import pallas as pl
from jax.experimental.pallas import tpu as pltpu

GRID_W = 64
N_MIXERS = 4
RMS_EPS = 1e-6
POOL_WINDOWS = (2, 4, 8, 16)
SGU_CHUNK = 128
POS_BASE = 10000.0

HALO = 8
TB = 512
FFN_TB = 256
FFN_FC = 256
HGRN_C = 64
HGRN_TB = 256
EXP_CLAMP = 80.0
F32_TINY = 1e-37
VMEM_LIMIT = 60000 * 1024

BF16 = jnp.bfloat16
F32 = jnp.float32


def _cparams(sem):
    return pltpu.CompilerParams(dimension_semantics=sem, vmem_limit_bytes=VMEM_LIMIT)


def _dot(a, b):
    return jnp.dot(a, b, preferred_element_type=F32)


def _const_spec(shape):
    nd = len(shape)
    return pl.BlockSpec(shape, lambda i: (0,) * nd)


def _layer_spec(stacked, idx):
    rest = stacked.shape[1:]
    return pl.BlockSpec((None,) + rest, lambda i: (idx,) + (0,) * len(rest))


class _Layout:
    def __init__(self, batch_p, seq_p, batch_s, seq_s, d):
        self.lp, self.ls, self.d = seq_p, seq_s, d
        self.np_rows = batch_p * seq_p
        self.ns_rows = batch_s * seq_s
        self.nt = self.np_rows + self.ns_rows
        self.batch_p, self.batch_s = batch_p, batch_s

    def check_block(self, tb):
        assert self.np_rows % tb == 0 and self.ns_rows % tb == 0
        for L in (self.lp, self.ls):
            assert L % tb == 0 or tb % L == 0

    def cond_row(self, r0):
        return jnp.where(r0 < self.np_rows, 0, 1 + (jnp.maximum(r0 - self.np_rows, 0)) // self.ls)


def _pos_in_seq(lay, r0, tb):
    k = lax.broadcasted_iota(jnp.int32, (tb, 1), 0)

    def pos_for(start, L):
        base = lax.rem(jnp.maximum(r0 - start, 0), L)
        p = base + k
        for j in range(1, (tb + L - 1) // L + 1):
            p = p - jnp.where(base + k >= j * L, L, 0)
        return p

    is_p = r0 < lay.np_rows
    pos = jnp.where(is_p, pos_for(0, lay.lp), pos_for(lay.np_rows, lay.ls))
    L = jnp.where(is_p, lay.lp, lay.ls)
    return pos, L


def _rms_mod(x, g, shift, scale):
    ms = jnp.mean(x * x, axis=-1, keepdims=True)
    return x * lax.rsqrt(ms + RMS_EPS) * (g * (1.0 + scale)) + shift


def _mod_parts(mod_ref, d, first):
    o = 0 if first else 3 * d
    return (mod_ref[:, o:o + d], mod_ref[:, o + d:o + 2 * d], mod_ref[:, o + 2 * d:o + 3 * d])


def _halo_specs(lay, tb):
    nb8 = lay.nt // HALO
    r = tb // HALO
    return [
        pl.BlockSpec((tb, lay.d), lambda i: (i, 0)),
        pl.BlockSpec((HALO, lay.d), lambda i: (jnp.maximum(i * r - 1, 0), 0)),
        pl.BlockSpec((HALO, lay.d), lambda i: (jnp.minimum((i + 1) * r, nb8 - 1), 0)),
    ]


def _mod_spec(lay, layer, tb):
    return pl.BlockSpec((None, 1, 6 * lay.d), lambda i: (layer * 8 + lay.cond_row(i * tb), 0, 0))


def _edge_groups(lay, tb):
    step = math.gcd(math.gcd(lay.lp, lay.ls), tb)
    assert step % HALO == 0
    firsts = list(range(0, tb, step))
    lasts = [(k - HALO) % tb for k in firsts]
    return firsts, lasts


def _zero_rows(a, mask, groups):
    pieces, cur = [], 0
    for r in sorted(groups):
        if r > cur:
            pieces.append(a[cur:r])
        pieces.append(jnp.where(mask[r:r + HALO], 0.0, a[r:r + HALO]))
        cur = r + HALO
    if cur < a.shape[0]:
        pieces.append(a[cur:])
    return jnp.concatenate(pieces, axis=0)


def _dwconv3(a_ref, w, first, last, tb, edges):
    prev = a_ref[HALO - 1:HALO - 1 + tb, :]
    mid = a_ref[HALO:HALO + tb, :]
    nxt = a_ref[HALO + 1:HALO + 1 + tb, :]
    prev = _zero_rows(prev, first, edges[0])
    nxt = _zero_rows(nxt, last, edges[1])
    return prev * w[0:1] + mid * w[1:2] + nxt * w[2:3]


def _pos_kernel(o_ref, *, tb, d):
    i = pl.program_id(0)
    q = d // 4
    nr = tb // GRID_W
    j = lax.broadcasted_iota(jnp.int32, (1, q), 1).astype(F32)
    freq = jnp.exp(-math.log(POS_BASE) * j / q)
    rr = (i * nr + lax.broadcasted_iota(jnp.int32, (nr, 1), 0)).astype(F32)
    cc = lax.broadcasted_iota(jnp.int32, (GRID_W, 1), 0).astype(F32)
    ar = rr * freq
    ac = cc * freq
    row_part = jnp.concatenate([jnp.sin(ar), jnp.cos(ar)], axis=1)
    col_part = jnp.concatenate([jnp.sin(ac), jnp.cos(ac)], axis=1)
    for r in range(nr):
        rows = slice(r * GRID_W, (r + 1) * GRID_W)
        o_ref[rows, 0:2 * q] = jnp.broadcast_to(row_part[r:r + 1], (GRID_W, 2 * q))
        o_ref[rows, 2 * q:4 * q] = col_part


def _pos_table(n_tokens, d):
    tb = 512
    assert tb % GRID_W == 0 and n_tokens % tb == 0
    return pl.pallas_call(
        functools.partial(_pos_kernel, tb=tb, d=d),
        out_shape=jax.ShapeDtypeStruct((n_tokens, d), F32),
        grid=(n_tokens // tb,),
        out_specs=pl.BlockSpec((tb, d), lambda i: (i, 0)),
        compiler_params=_cparams(("parallel",)),
        name="pos_table",
    )()


def _ada_kernel(c_ref, w_ref, b_ref, o_ref):
    c = c_ref[...]
    s = (c * jax.nn.sigmoid(c)).astype(BF16)
    o_ref[...] = _dot(s, w_ref[...].astype(BF16)) + b_ref[...]


def _ada_params(cvec, ada_w, ada_b):
    depth, d, n = ada_w.shape
    tn = 1536
    assert n % tn == 0
    return pl.pallas_call(
        _ada_kernel,
        out_shape=jax.ShapeDtypeStruct((depth, 8, n), F32),
        grid=(depth, n // tn),
        in_specs=[
            pl.BlockSpec((8, d), lambda l, j: (0, 0)),
            pl.BlockSpec((None, d, tn), lambda l, j: (l, 0, j)),
            pl.BlockSpec((None, 1, tn), lambda l, j: (l, 0, j)),
        ],
        out_specs=pl.BlockSpec((None, 8, tn), lambda l, j: (l, 0, j)),
        compiler_params=_cparams(("parallel", "parallel")),
        name="ada_params",
    )(cvec, ada_w, ada_b.reshape(depth, 1, n))


def _edge_points(lay, r0, tb):
    is_p = r0 < lay.np_rows
    base = jnp.where(is_p, lax.rem(r0, lay.lp), lax.rem(jnp.maximum(r0 - lay.np_rows, 0), lay.ls))
    L = jnp.where(is_p, lay.lp, lay.ls)
    step = math.gcd(math.gcd(lay.lp, lay.ls), tb)
    firsts = [(k, lax.rem(base + k, L) == 0) for k in range(0, tb, step)]
    lasts = [((k - 1) % tb, lax.rem(base + ((k - 1) % tb) + 1, L) == 0) for k in range(0, tb, step)]
    return firsts, lasts


def _zero_points(a, points, s_rows):
    sub = lax.broadcasted_iota(jnp.int32, (HALO, 1), 0)
    pieces, cur = [], 0
    for k, flag in sorted(points, key=lambda p: (p[0] + HALO) % s_rows):
        s, j = divmod(k + HALO, s_rows)
        r = HALO * j
        assert r >= cur
        if r > cur:
            pieces.append(a[cur:r])
        pieces.append(jnp.where(jnp.logical_and(sub == s, flag), 0.0, a[r:r + HALO]))
        cur = r + HALO
    if cur < a.shape[0]:
        pieces.append(a[cur:])
    return jnp.concatenate(pieces, axis=0)


def _dwconv3_strided(u_ref, w, firsts, lasts):
    n = u_ref.shape[0]
    s_rows = n // HALO
    mid = u_ref[...]
    prev = jnp.concatenate([pltpu.roll(u_ref[n - HALO:n, :], 1, axis=0), u_ref[0:n - HALO, :]], axis=0)
    nxt = jnp.concatenate([u_ref[HALO:n, :], pltpu.roll(u_ref[0:HALO, :], HALO - 1, axis=0)], axis=0)
    prev = _zero_points(prev, firsts, s_rows)
    nxt = _zero_points(nxt, lasts, s_rows)
    return prev * w[0:1] + mid * w[1:2] + nxt * w[2:3]


def _ffn_kernel(*refs, lay, tb, nf, final, heads):
    refs = list(refs)
    x_ref, xp_ref, xn_ref, mod_ref, g_ref, wup_ref, wdw_ref, wdn_ref = refs[:8]
    del refs[:8]
    if heads:
        mix_refs, ng_ref, wmix_ref = refs[:9], refs[9], refs[10]
        del refs[:11]
    if final:
        fg_ref, yp_ref, ys_ref = refs[:3]
        del refs[:3]
    else:
        o_ref = refs.pop(0)
    slab_ref, u_ref = refs[:2]
    d = lay.d
    fc = FFN_FC
    n = tb + 2 * HALO
    s_rows = n // HALO
    nl = d // 128
    i = pl.program_id(0)
    firsts, lasts = _edge_points(lay, i * tb, tb)
    shift, scale, gate = _mod_parts(mod_ref, d, first=False)

    if heads:
        x1_ref = refs[2]
        dk = d // heads

        def rows3(j):
            m_ref, p_ref, nx_ref = mix_refs[3 * j:3 * j + 3]
            return jnp.concatenate([p_ref[...].astype(F32)[HALO:2 * HALO], m_ref[...].astype(F32),
                                    nx_ref[...].astype(F32)[0:HALO]], axis=0)

        o = rows3(0) + rows3(1)
        parts = []
        for hd in range(heads):
            oh = o[:, hd * dk:(hd + 1) * dk]
            parts.append(oh * lax.rsqrt(jnp.mean(oh * oh, axis=-1, keepdims=True) + RMS_EPS))
        on = jnp.concatenate(parts, axis=-1) * ng_ref[...]
        y1 = _dot((on * rows3(2)).astype(BF16), wmix_ref[...])
        xh = jnp.concatenate([xp_ref[...], x_ref[...], xn_ref[...]], axis=0)
        x1_ref[...] = xh + mod_ref[:, 2 * d:3 * d] * y1
        x_main = x1_ref.at[HALO:HALO + tb, :]
    else:
        x_main = x_ref

    for l in range(nl):
        cols = slice(128 * l, 128 * (l + 1))
        if heads:
            slab_ref[l] = x1_ref[:, cols]
        else:
            slab_ref[l, 0:HALO, :] = xp_ref[:, cols]
            slab_ref[l, HALO:HALO + tb, :] = x_ref[:, cols]
            slab_ref[l, HALO + tb:n, :] = xn_ref[:, cols]
    xs = jnp.concatenate(
        [jnp.concatenate([slab_ref[l, pl.ds(j, HALO, stride=s_rows), :] for l in range(nl)], axis=1)
         for j in range(s_rows)], axis=0)
    h = _rms_mod(xs, g_ref[...], shift, scale).astype(BF16)

    def up(c):
        for ab in range(2):
            u_ref[c % 2, ab] = _dot(h, wup_ref[:, ab * nf * fc + c * fc:ab * nf * fc + (c + 1) * fc])

    up(0)
    acc = None
    for c in range(nf):
        if c + 1 < nf:
            up(c + 1)
        ca = _dwconv3_strided(u_ref.at[c % 2, 0], wdw_ref[:, c * fc:(c + 1) * fc], firsts, lasts)
        cb = _dwconv3_strided(u_ref.at[c % 2, 1], wdw_ref[:, (nf + c) * fc:(nf + c + 1) * fc], firsts, lasts)
        act = (ca * jax.nn.sigmoid(ca) * cb).astype(BF16)
        part = _dot(act, wdn_ref[c * fc:(c + 1) * fc, :])
        acc = part if acc is None else acc + part

    for l in range(nl):
        for j in range(s_rows):
            slab_ref[l, pl.ds(j, HALO, stride=s_rows), :] = acc[HALO * j:HALO * (j + 1), 128 * l:128 * (l + 1)]
    y = jnp.concatenate([slab_ref[l, HALO:HALO + tb, :] for l in range(nl)], axis=1)
    out = x_main[...] + gate * y
    if not final:
        o_ref[...] = out
    else:
        ms = jnp.mean(out * out, axis=-1, keepdims=True)
        out = out * lax.rsqrt(ms + RMS_EPS) * fg_ref[...]
        is_p = i * tb < lay.np_rows

        @pl.when(is_p)
        def _():
            yp_ref[...] = out

        @pl.when(jnp.logical_not(is_p))
        def _():
            ys_ref[...] = out


def _ffn(lay, x, mods, layer, g, wup, wdw, wdn, final_g=None, hgrn=None):
    tb = FFN_TB
    d = lay.d
    mix_specs, mix_args, heads = [], [], 0
    if hgrn is not None:
        o_f, o_b, sg, ng, w_mix, j, heads = hgrn
        h16 = 2 * HALO
        r16 = tb // h16
        last16 = lay.nt // h16 - 1
        for a in (o_f, o_b, sg):
            mix_specs += [pl.BlockSpec((tb, d), lambda i: (i, 0)),
                          pl.BlockSpec((h16, d), lambda i: (jnp.maximum(i * r16 - 1, 0), 0)),
                          pl.BlockSpec((h16, d), lambda i: (jnp.minimum((i + 1) * r16, last16), 0))]
            mix_args += [a, a, a]
        mix_specs += [_const_spec((1, d)), _layer_spec(w_mix, j)]
        mix_args += [ng, w_mix]
    assert wdn.shape[1] % FFN_FC == 0
    nf = wdn.shape[1] // FFN_FC
    n = tb + 2 * HALO
    lay.check_block(tb)
    assert n % HALO == 0 and d % 128 == 0
    final = final_g is not None
    blk = pl.BlockSpec((tb, d), lambda i: (i, 0))
    if final:
        npb = lay.np_rows // tb
        out_shape = [jax.ShapeDtypeStruct((lay.np_rows, d), F32), jax.ShapeDtypeStruct((lay.ns_rows, d), F32)]
        out_specs = [pl.BlockSpec((tb, d), lambda i: (jnp.minimum(i, npb - 1), 0)),
                     pl.BlockSpec((tb, d), lambda i: (jnp.maximum(i - npb, 0), 0))]
    else:
        out_shape = jax.ShapeDtypeStruct((lay.nt, d), F32)
        out_specs = blk
    return pl.pallas_call(
        functools.partial(_ffn_kernel, lay=lay, tb=tb, nf=nf, final=final, heads=heads),
        out_shape=out_shape,
        grid=(lay.nt // tb,),
        in_specs=_halo_specs(lay, tb) + [
            _mod_spec(lay, layer, tb),
            _const_spec((1, d)),
            _layer_spec(wup, layer),
            _layer_spec(wdw, layer),
            _layer_spec(wdn, layer),
        ] + mix_specs + ([_const_spec((1, d))] if final else []),
        out_specs=out_specs,
        scratch_shapes=[pltpu.VMEM((d // 128, n, 128), F32),
                        pltpu.VMEM((2, 2, n, FFN_FC), F32)] + ([pltpu.VMEM((n, d), F32)] if heads else []),
        compiler_params=_cparams(("arbitrary",)),
        name="conv_ffn",
    )(x, x, x, mods, g, wup, wdw, wdn, *mix_args, *([final_g] if final else []))


def _sconv_kernel(*refs, lay, tb, embed):
    if embed:
        (xp_ref, xpp_ref, xpn_ref, xs_ref, xsp_ref, xsn_ref, ps_ref, psp_ref, psn_ref,
         mod_ref, g_ref, win_ref, wdw_ref, wout_ref, o_ref, xh_ref, p_ref) = refs
    else:
        x_ref, xp_ref, xn_ref, mod_ref, g_ref, win_ref, wdw_ref, wout_ref, o_ref, xh_ref, p_ref = refs
    d = lay.d
    n = tb + 2 * HALO
    i = pl.program_id(0)
    pos, L = _pos_in_seq(lay, i * tb, tb)
    first, last = pos == 0, pos == L - 1
    shift, scale, gate = _mod_parts(mod_ref, d, first=True)
    if embed:
        is_p = i * tb < lay.np_rows

        @pl.when(is_p)
        def _():
            xh_ref[0:HALO, :] = xpp_ref[...]
            xh_ref[HALO:HALO + tb, :] = xp_ref[...]
            xh_ref[HALO + tb:n, :] = xpn_ref[...]

        @pl.when(jnp.logical_not(is_p))
        def _():
            xh_ref[0:HALO, :] = xsp_ref[...] + psp_ref[...]
            xh_ref[HALO:HALO + tb, :] = xs_ref[...] + ps_ref[...]
            xh_ref[HALO + tb:n, :] = xsn_ref[...] + psn_ref[...]
    else:
        xh_ref[0:HALO, :] = xp_ref[...]
        xh_ref[HALO:HALO + tb, :] = x_ref[...]
        xh_ref[HALO + tb:n, :] = xn_ref[...]
    h = _rms_mod(xh_ref[...], g_ref[...], shift, scale).astype(BF16)
    bg = _dot(h, win_ref[:, 0:d])[HALO:HALO + tb]
    p_ref[...] = _dot(h, win_ref[:, d:2 * d]) * _dot(h, win_ref[:, 2 * d:3 * d])
    conv = _dwconv3(p_ref, wdw_ref[...], first, last, tb, _edge_groups(lay, tb))
    y = _dot((bg * conv).astype(BF16), wout_ref[...])
    o_ref[...] = xh_ref[HALO:HALO + tb, :] + gate * y


def _stream_halo_specs(rows, d, tb, blk_of):
    r = tb // HALO
    last8 = rows // HALO - 1
    return [
        pl.BlockSpec((tb, d), lambda i: (blk_of(i), 0)),
        pl.BlockSpec((HALO, d), lambda i: (jnp.clip(blk_of(i) * r - 1, 0, last8), 0)),
        pl.BlockSpec((HALO, d), lambda i: (jnp.clip((blk_of(i) + 1) * r, 0, last8), 0)),
    ]


def _sconv_layer(lay, x, mods, layer, g, j, w_in, w_dw, w_out, raw=None):
    tb = TB
    d = lay.d
    embed = raw is not None
    if embed:
        npb = lay.np_rows // tb
        psb = lay.ls // tb
        xspecs = (_stream_halo_specs(lay.np_rows, d, tb, lambda i: jnp.minimum(i, npb - 1))
                  + _stream_halo_specs(lay.ns_rows, d, tb, lambda i: jnp.maximum(i - npb, 0))
                  + _stream_halo_specs(lay.ls, d, tb, lambda i: lax.rem(jnp.maximum(i - npb, 0), psb)))
        xargs = (raw[0],) * 3 + (raw[1],) * 3 + (raw[2],) * 3
    else:
        xspecs = _halo_specs(lay, tb)
        xargs = (x, x, x)
    return pl.pallas_call(
        functools.partial(_sconv_kernel, lay=lay, tb=tb, embed=embed),
        out_shape=jax.ShapeDtypeStruct((lay.nt, d), F32),
        grid=(lay.nt // tb,),
        in_specs=xspecs + [
            _mod_spec(lay, layer, tb),
            _const_spec((1, d)),
            _layer_spec(w_in, j),
            _layer_spec(w_dw, j),
            _layer_spec(w_out, j),
        ],
        out_specs=pl.BlockSpec((tb, d), lambda i: (i, 0)),
        scratch_shapes=[pltpu.VMEM((tb + 2 * HALO, d), F32), pltpu.VMEM((tb + 2 * HALO, d), F32)],
        compiler_params=_cparams(("arbitrary",)),
        name="sconv_mixer",
    )(*xargs, mods, g, w_in, w_dw, w_out)


def _pool_kernel(x_ref, xp_ref, xn_ref, mod_ref, g_ref, w_ref, sc_ref, o_ref, h_ref, *, lay, tb):
    d = lay.d
    ng = len(POOL_WINDOWS)
    pg = d // ng
    i = pl.program_id(0)
    pos, L = _pos_in_seq(lay, i * tb, tb)
    shift, scale, gate = _mod_parts(mod_ref, d, first=True)
    xh = jnp.concatenate([xp_ref[...], x_ref[...], xn_ref[...]], axis=0)
    h_ref[...] = _rms_mod(xh, g_ref[...], shift, scale)
    n = tb + 2 * HALO
    edge_groups = sorted(set(_edge_groups(lay, tb)[0]) | set(_edge_groups(lay, tb)[1]))
    outs = []
    for gi, w in enumerate(POOL_WINDOWS):
        assert w // 2 <= HALO and w & (w - 1) == 0
        cols = slice(gi * pg, (gi + 1) * pg)
        centre = h_ref[HALO:HALO + tb, cols]
        p = h_ref[:, cols]
        m = 1
        while m < w:
            p = p + pltpu.roll(p, m, axis=0)
            m *= 2
        ahead = w // 2 - 1
        if ahead:
            p = pltpu.roll(p, n - ahead, axis=0)
        mean = p[HALO:HALO + tb] * (1.0 / w)
        pieces, cur = [], 0
        for r in edge_groups:
            if r > cur:
                pieces.append(mean[cur:r])
            pr = pos[r:r + HALO]
            s = jnp.zeros((HALO, pg), F32)
            cnt = jnp.zeros((HALO, 1), F32)
            for k in range(-(w // 2), w // 2):
                inside = jnp.logical_and(pr + k >= 0, pr + k <= L - 1)
                s = s + jnp.where(inside, h_ref[HALO + r + k:HALO + r + k + HALO, cols], 0.0)
                cnt = cnt + jnp.where(inside, 1.0, 0.0)
            pieces.append(s / cnt)
            cur = r + HALO
        if cur < tb:
            pieces.append(mean[cur:])
        mean = jnp.concatenate(pieces, axis=0)
        pooled = (mean - centre).astype(BF16)
        outs.append(_dot(pooled, w_ref[gi]))
    y = jnp.concatenate(outs, axis=-1) * sc_ref[...]
    o_ref[...] = x_ref[...] + gate * y


def _pool_layer(lay, x, mods, layer, g, j, w, sc):
    tb = TB
    d = lay.d
    return pl.pallas_call(
        functools.partial(_pool_kernel, lay=lay, tb=tb),
        out_shape=jax.ShapeDtypeStruct((lay.nt, d), F32),
        grid=(lay.nt // tb,),
        in_specs=_halo_specs(lay, tb) + [
            _mod_spec(lay, layer, tb),
            _const_spec((1, d)),
            _layer_spec(w, j),
            _const_spec((1, d)),
        ],
        out_specs=pl.BlockSpec((tb, d), lambda i: (i, 0)),
        scratch_shapes=[pltpu.VMEM((tb + 2 * HALO, d), F32)],
        compiler_params=_cparams(("parallel",)),
        name="pool_mixer",
    )(x, x, x, mods, g, w, sc)


def _sgu_kernel(x_ref, mod_ref, g_ref, win_ref, ng_ref, ws_ref, bs_ref, wout_ref, o_ref, s_ref, *, lay, tb):
    d = lay.d
    groups = ws_ref.shape[0]
    gd = d // groups
    shift, scale, gate = _mod_parts(mod_ref, d, first=True)
    x = x_ref[...]
    h = _rms_mod(x, g_ref[...], shift, scale).astype(BF16)
    v = jax.nn.gelu(_dot(h, win_ref[:, d:2 * d]), approximate=True)
    u = jax.nn.gelu(_dot(h, win_ref[:, 0:d]), approximate=True)
    ms = jnp.mean(v * v, axis=-1, keepdims=True)
    vb = (v * lax.rsqrt(ms + RMS_EPS) * ng_ref[...]).astype(BF16)
    for n in range(tb // SGU_CHUNK):
        for gi in range(groups):
            rows = slice(n * SGU_CHUNK, (n + 1) * SGU_CHUNK)
            cols = slice(gi * gd, (gi + 1) * gd)
            s_ref[rows, cols] = _dot(ws_ref[gi], vb[rows, cols]) + bs_ref[:, gi:gi + 1]
    y = _dot((u * s_ref[...]).astype(BF16), wout_ref[...])
    o_ref[...] = x + gate * y


def _sgu_layer(lay, x, mods, layer, g, j, w_in, norm_g, w_s, b_st, w_out):
    tb = TB
    d = lay.d
    assert tb % SGU_CHUNK == 0 and lay.lp % SGU_CHUNK == 0 and lay.ls % SGU_CHUNK == 0
    return pl.pallas_call(
        functools.partial(_sgu_kernel, lay=lay, tb=tb),
        out_shape=jax.ShapeDtypeStruct((lay.nt, d), F32),
        grid=(lay.nt // tb,),
        in_specs=[
            pl.BlockSpec((tb, d), lambda i: (i, 0)),
            _mod_spec(lay, layer, tb),
            _const_spec((1, d)),
            _layer_spec(w_in, j),
            _const_spec((1, d)),
            _layer_spec(w_s, j),
            _const_spec(b_st.shape),
            _layer_spec(w_out, j),
        ],
        out_specs=pl.BlockSpec((tb, d), lambda i: (i, 0)),
        scratch_shapes=[pltpu.VMEM((tb, d), F32)],
        compiler_params=_cparams(("parallel",)),
        name="sgu_mixer",
    )(x, mods, g, w_in, norm_g, w_s, b_st, w_out)


def _split2(a):
    hi = a.astype(BF16)
    lo = (a - hi.astype(F32)).astype(BF16)
    return hi, lo


def _hgrn_gates(z, lb, one_m_lb):
    ez = jnp.exp(-jnp.abs(z))
    r = 1.0 / (1.0 + ez)
    er = ez * r
    pos = z >= 0.0
    f = lb + one_m_lb * jnp.where(pos, r, er)
    logf = jnp.log(jnp.maximum(f, F32_TINY))
    return logf, one_m_lb * jnp.where(pos, er, r)


def _hgrn_proj_kernel(x_ref, mod_ref, g_ref, win_ref, lb_ref,
                      qf_ref, kf_ref, qb_ref, kb_ref, v_ref, sg_ref, cv_ref, *, lay, tb, layer):
    d = lay.d
    c = HGRN_C
    shift, scale, _ = _mod_parts(mod_ref, d, first=True)
    h = _rms_mod(x_ref[...], g_ref[...], shift, scale).astype(BF16)

    lbp = lb_ref[...]
    e = jnp.exp(lbp - jnp.max(lbp, axis=0, keepdims=True))
    p = e / jnp.sum(e, axis=0, keepdims=True)
    lb = jnp.zeros((1, 2 * d), F32)
    for j in range(1, layer + 1):
        lb = lb + p[j:j + 1]
    one_m_lb = 1.0 - lb

    z = [_dot(h, win_ref[:, (1 + di) * d:(2 + di) * d]) for di in range(2)]
    q = _dot(h, win_ref[:, 0:d])
    gates = [_hgrn_gates(z[di], lb[:, di * d:(di + 1) * d], one_m_lb[:, di * d:(di + 1) * d]) for di in range(2)]

    rt = lax.broadcasted_iota(jnp.int32, (c, 2 * c), 0)
    cs = lax.broadcasted_iota(jnp.int32, (c, 2 * c), 1)
    cs = jnp.where(cs >= c, cs - c, cs)
    tri_f = (cs <= rt).astype(BF16)
    tri_b = (cs >= rt).astype(BF16)
    mid = c // 2

    def cumsum(di, tri):
        logf = gates[di][0]
        return [_dot(tri, jnp.concatenate(_split2(logf[n * c:(n + 1) * c]), axis=0)) for n in range(tb // c)]

    cum = [cumsum(0, tri_f)]
    vv = _dot(h, win_ref[:, 3 * d:4 * d])
    cum.append(cumsum(1, tri_b))
    gg = _dot(h, win_ref[:, 4 * d:5 * d])

    for di, (q_ref, k_ref) in enumerate(((qf_ref, kf_ref), (qb_ref, kb_ref))):
        kk = gates[di][1]
        for n in range(tb // c):
            rows = slice(n * c, (n + 1) * c)
            b = cum[di][n]
            if di == 0:
                btot = b[c - 1:c]
                bref = b[mid - 1:mid]
            else:
                btot = b[0:1]
                bref = b[mid:mid + 1]
            grow = jnp.exp(jnp.clip(b - bref, -EXP_CLAMP, EXP_CLAMP))
            q_ref[rows, :] = (q[rows] * grow).astype(BF16)
            k_ref[rows, :] = (kk[rows] * pl.reciprocal(grow, approx=True)).astype(BF16)
            cv_ref[n, 3 * di:3 * di + 1, :] = jnp.exp(bref)
            cv_ref[n, 3 * di + 1:3 * di + 2, :] = jnp.exp(btot - bref)
            cv_ref[n, 3 * di + 2:3 * di + 3, :] = jnp.exp(btot)
    v_ref[...] = vv.astype(BF16)
    sg_ref[...] = (gg * jax.nn.sigmoid(gg)).astype(BF16)
    for n in range(tb // c):
        cv_ref[n, 6:8, :] = jnp.zeros((2, d), F32)


def _hgrn_proj(lay, x, mods, layer, g, j, w_in, lb2):
    tb = TB
    d = lay.d
    c = HGRN_C
    blk = pl.BlockSpec((tb, d), lambda i: (i, 0))
    act = jax.ShapeDtypeStruct((lay.nt, d), BF16)
    return pl.pallas_call(
        functools.partial(_hgrn_proj_kernel, lay=lay, tb=tb, layer=layer),
        out_shape=[act, act, act, act, act, act, jax.ShapeDtypeStruct((lay.nt // c, 8, d), F32)],
        grid=(lay.nt // tb,),
        in_specs=[blk, _mod_spec(lay, layer, tb), _const_spec((1, d)), _layer_spec(w_in, j),
                  _const_spec(lb2.shape)],
        out_specs=[blk, blk, blk, blk, blk, blk, pl.BlockSpec((tb // c, 8, d), lambda i: (i, 0, 0))],
        compiler_params=_cparams(("parallel",)),
        name="hgrn_proj",
    )(x, mods, g, w_in, lb2)


def _hgrn_scan_kernel(qf_ref, kf_ref, vf_ref, cvf_ref, qb_ref, kb_ref, vb_ref, cvb_ref, s0_ref,
                      of_ref, ob_ref, sfin_ref, st_ref, *, lay, tb, heads):
    d = lay.d
    c = HGRN_C
    dk = d // heads
    i = pl.program_id(0)
    r0 = i * tb
    is_p = r0 < lay.np_rows
    base = jnp.where(is_p, lax.rem(r0, lay.lp), lax.rem(jnp.maximum(r0 - lay.np_rows, 0), lay.ls))
    L = jnp.where(is_p, lay.lp, lay.ls)

    @pl.when(jnp.logical_and(base == 0, is_p))
    def _():
        st_ref[...] = jnp.zeros_like(st_ref)

    @pl.when(jnp.logical_and(base == 0, jnp.logical_not(is_p)))
    def _():
        for di in range(2):
            for hd in range(heads):
                st_ref[di, hd] = s0_ref[di, hd].T

    rt = lax.broadcasted_iota(jnp.int32, (c, c), 0)
    cs = lax.broadcasted_iota(jnp.int32, (c, c), 1)
    keep = (cs <= rt, cs >= rt)
    nchunk = tb // c
    dirs = ((qf_ref, kf_ref, vf_ref, cvf_ref, of_ref), (qb_ref, kb_ref, vb_ref, cvb_ref, ob_ref))
    nt_dims = (((1,), (1,)), ((), ()))
    tn_dims = (((0,), (0,)), ((), ()))
    st = [[st_ref[di, hd] for hd in range(heads)] for di in range(2)]
    for n in range(nchunk):
        part = {}
        for di, (q_ref, k_ref, v_ref, cv_ref, _) in enumerate(dirs):
            nn = n if di == 0 else nchunk - 1 - n
            rows = slice(nn * c, (nn + 1) * c)
            for hd in range(heads):
                cols = slice(hd * dk, (hd + 1) * dk)
                qs, ks, vv = q_ref[rows, cols], k_ref[rows, cols], v_ref[rows, cols]
                ku = (ks.astype(F32) * cv_ref[nn, 3 * di + 1:3 * di + 2, cols]).astype(BF16)
                sc = lax.dot_general(qs, ks, nt_dims, preferred_element_type=F32)
                upd = lax.dot_general(vv, ku, tn_dims, preferred_element_type=F32)
                part[di, hd] = (qs, vv, sc, upd)
        for di, (_, _, _, cv_ref, o_ref) in enumerate(dirs):
            nn = n if di == 0 else nchunk - 1 - n
            rows = slice(nn * c, (nn + 1) * c)
            for hd in range(heads):
                cols = slice(hd * dk, (hd + 1) * dk)
                qs, vv, sc, upd = part[di, hd]
                scb = jnp.where(keep[di], sc, 0.0).astype(BF16)
                qi = (qs.astype(F32) * cv_ref[nn, 3 * di:3 * di + 1, cols]).astype(BF16)
                o_ref[rows, cols] = (_dot(scb, vv) + lax.dot_general(
                    qi, st[di][hd].astype(BF16), nt_dims, preferred_element_type=F32)).astype(BF16)
                st[di][hd] = st[di][hd] * cv_ref[nn, 3 * di + 2:3 * di + 3, cols] + upd
    for di in range(2):
        for hd in range(heads):
            st_ref[di, hd] = st[di][hd]

    @pl.when(jnp.logical_and(base + tb == L, is_p))
    def _():
        for di in range(2):
            for hd in range(heads):
                sfin_ref[di, hd] = st_ref[di, hd].T


def _hgrn_scan(lay, qf, kf, qb, kb, v, cv, s0, heads):
    tb = HGRN_TB
    d = lay.d
    c = HGRN_C
    dk = d // heads
    lay.check_block(tb)
    assert tb <= lay.lp and tb <= lay.ls
    npb = lay.np_rows // tb
    bps_p, bps_s = lay.lp // tb, lay.ls // tb

    def mirror(i):
        ip = (i // bps_p) * bps_p + (bps_p - 1 - lax.rem(i, bps_p))
        j = jnp.maximum(i - npb, 0)
        isx = npb + (j // bps_s) * bps_s + (bps_s - 1 - lax.rem(j, bps_s))
        return jnp.where(i < npb, ip, isx)

    def seq_of(i):
        return jnp.where(i < npb, i // bps_p, lay.batch_p + jnp.maximum(i - npb, 0) // bps_s)

    fwd = pl.BlockSpec((tb, d), lambda i: (i, 0))
    bwd = pl.BlockSpec((tb, d), lambda i: (mirror(i), 0))
    cvf = pl.BlockSpec((tb // c, 8, d), lambda i: (i, 0, 0))
    cvb = pl.BlockSpec((tb // c, 8, d), lambda i: (mirror(i), 0, 0))
    st_blk = (None, 2, heads, dk, dk)
    s0_spec = pl.BlockSpec(st_blk, lambda i: (jnp.clip(seq_of(i) - lay.batch_p, 0, lay.batch_s - 1), 0, 0, 0, 0))
    sfin_spec = pl.BlockSpec(st_blk, lambda i: (jnp.minimum(seq_of(i), lay.batch_p - 1), 0, 0, 0, 0))
    return pl.pallas_call(
        functools.partial(_hgrn_scan_kernel, lay=lay, tb=tb, heads=heads),
        out_shape=[jax.ShapeDtypeStruct((lay.nt, d), BF16), jax.ShapeDtypeStruct((lay.nt, d), BF16),
                   jax.ShapeDtypeStruct((lay.batch_p, 2, heads, dk, dk), F32)],
        grid=(lay.nt // tb,),
        in_specs=[fwd, fwd, fwd, cvf, bwd, bwd, bwd, cvb, s0_spec],
        out_specs=[fwd, bwd, sfin_spec],
        scratch_shapes=[pltpu.VMEM((2, heads, dk, dk), F32)],
        compiler_params=_cparams(("arbitrary",)),
        name="hgrn_scan",
    )(qf, kf, v, cv, qb, kb, v, cv, s0)


def kernel(x_prompt, x_sample, state_rec, c, c_ctx, ada_w, ada_b, norm_g, final_g, conv_w_in, conv_w_dw,
           conv_w_out, pool_w, pool_scale, sgu_w_in, sgu_norm_g, sgu_w_s, sgu_b_s, sgu_w_out, hgrn_w_in,
           hgrn_lb, hgrn_norm_g, hgrn_w_out, ffn_w_up, ffn_w_dw, ffn_w_down):
    bp, lp, d = x_prompt.shape
    bs, ls, _ = x_sample.shape
    depth = ada_w.shape[0]
    heads = state_rec.shape[3]
    lay = _Layout(bp, lp, bs, ls, d)
    lay.check_block(TB)
    assert c.shape[0] == bs and 1 + bs <= 8

    cvec = jnp.concatenate([c_ctx[None], c, jnp.zeros((8 - 1 - bs, d), F32)], axis=0)
    mods = _ada_params(cvec, ada_w, ada_b).reshape(depth * 8, 1, 6 * d)

    assert N_MIXERS >= 1 and depth >= 1
    raw = (x_prompt.reshape(bp * lp, d), x_sample.reshape(bs * ls, d), _pos_table(ls, d))
    x = None

    wb = {k: v.astype(BF16) for k, v in dict(
        conv_in=conv_w_in, conv_out=conv_w_out, pool=pool_w, sgu_in=sgu_w_in, sgu_s=sgu_w_s, sgu_out=sgu_w_out,
        hgrn_in=hgrn_w_in, hgrn_out=hgrn_w_out, ffn_up=ffn_w_up, ffn_down=ffn_w_down).items()}

    new_states = []
    for i in range(depth):
        kind, j = i % N_MIXERS, i // N_MIXERS
        g1 = norm_g[i, 0].reshape(1, d)
        g2 = norm_g[i, 1].reshape(1, d)
        hgrn = None
        if kind == 0:
            x = _sconv_layer(lay, x, mods, i, g1, j, wb["conv_in"], conv_w_dw, wb["conv_out"],
                             raw=raw if i == 0 else None)
        elif kind == 1:
            x = _pool_layer(lay, x, mods, i, g1, j, wb["pool"], pool_scale[j].reshape(1, d))
        elif kind == 2:
            x = _sgu_layer(lay, x, mods, i, g1, j, wb["sgu_in"], sgu_norm_g[j].reshape(1, d),
                           wb["sgu_s"], sgu_b_s[j].T, wb["sgu_out"])
        else:
            qf, kf, qb, kb, v, sg, cv = _hgrn_proj(lay, x, mods, i, g1, j, wb["hgrn_in"],
                                                   hgrn_lb.reshape(depth, 2 * d))
            o_f, o_b, sfin = _hgrn_scan(lay, qf, kf, qb, kb, v, cv, state_rec[:, j], heads)
            new_states.append(sfin)
            hgrn = (o_f, o_b, sg, hgrn_norm_g[j].reshape(1, d), wb["hgrn_out"], j, heads)
        ffn_args = (lay, x, mods, i, g2, wb["ffn_up"], ffn_w_dw, wb["ffn_down"])
        if i + 1 < depth:
            x = _ffn(*ffn_args, hgrn=hgrn)
        else:
            y_prompt, y_sample = _ffn(*ffn_args, final_g=final_g.reshape(1, d), hgrn=hgrn)
    y_prompt = y_prompt.reshape(bp, lp, d)
    y_sample = y_sample.reshape(bs, ls, d)
    new_state_rec = jnp.stack(new_states, axis=1)
    return (y_prompt, y_sample, new_state_rec)
```

```python
import functools
import math

import jax
import jax.numpy as jnp
from jax import lax
from jax.experimental import pallas as pl
from jax.experimental.pallas import tpu as pltpu

GRID_W = 64
N_MIXERS = 4
RMS_EPS = 1e-6
POOL_WINDOWS = (2, 4, 8, 16)
SGU_CHUNK = 128
POS_BASE = 10000.0

HALO = 8
TB = 512
FFN_TB = 256
FFN_NSUB = 2
FFN_FC = 256
HGRN_C = 64
HGRN_TB = 256
EXP_CLAMP = 80.0
F32_TINY = 1e-37
VMEM_LIMIT = 60000 * 1024

BF16 = jnp.bfloat16
F32 = jnp.float32


def _cparams(sem):
    return pltpu.CompilerParams(dimension_semantics=sem, vmem_limit_bytes=VMEM_LIMIT)


def _dot(a, b):
    return jnp.dot(a, b, preferred_element_type=F32)


def _const_spec(shape):
    nd = len(shape)
    return pl.BlockSpec(shape, lambda i: (0,) * nd)


def _layer_spec(stacked, idx):
    rest = stacked.shape[1:]
    return pl.BlockSpec((None,) + rest, lambda i: (idx,) + (0,) * len(rest))


class _Layout:
    def __init__(self, batch_p, seq_p, batch_s, seq_s, d):
        self.lp, self.ls, self.d = seq_p, seq_s, d
        self.np_rows = batch_p * seq_p
        self.ns_rows = batch_s * seq_s
        self.nt = self.np_rows + self.ns_rows
        self.batch_p, self.batch_s = batch_p, batch_s

    def check_block(self, tb):
        assert self.np_rows % tb == 0 and self.ns_rows % tb == 0
        for L in (self.lp, self.ls):
            assert L % tb == 0 or tb % L == 0

    def cond_row(self, r0):
        return jnp.where(r0 < self.np_rows, 0, 1 + (jnp.maximum(r0 - self.np_rows, 0)) // self.ls)


def _pos_in_seq(lay, r0, tb):
    k = lax.broadcasted_iota(jnp.int32, (tb, 1), 0)

    def pos_for(start, L):
        base = lax.rem(jnp.maximum(r0 - start, 0), L)
        p = base + k
        for j in range(1, (tb + L - 1) // L + 1):
            p = p - jnp.where(base + k >= j * L, L, 0)
        return p

    is_p = r0 < lay.np_rows
    pos = jnp.where(is_p, pos_for(0, lay.lp), pos_for(lay.np_rows, lay.ls))
    L = jnp.where(is_p, lay.lp, lay.ls)
    return pos, L


def _rms_mod(x, g, shift, scale):
    ms = jnp.mean(x * x, axis=-1, keepdims=True)
    return x * lax.rsqrt(ms + RMS_EPS) * (g * (1.0 + scale)) + shift


def _mod_parts(mod_ref, d, first):
    o = 0 if first else 3 * d
    return (mod_ref[:, o:o + d], mod_ref[:, o + d:o + 2 * d], mod_ref[:, o + 2 * d:o + 3 * d])


def _halo_specs(lay, tb):
    nb8 = lay.nt // HALO
    r = tb // HALO
    return [
        pl.BlockSpec((tb, lay.d), lambda i: (i, 0)),
        pl.BlockSpec((HALO, lay.d), lambda i: (jnp.maximum(i * r - 1, 0), 0)),
        pl.BlockSpec((HALO, lay.d), lambda i: (jnp.minimum((i + 1) * r, nb8 - 1), 0)),
    ]


def _mod_spec(lay, layer, tb):
    return pl.BlockSpec((None, 1, 6 * lay.d), lambda i: (layer * 8 + lay.cond_row(i * tb), 0, 0))


def _edge_groups(lay, tb):
    step = math.gcd(math.gcd(lay.lp, lay.ls), tb)
    assert step % HALO == 0
    firsts = list(range(0, tb, step))
    lasts = [(k - HALO) % tb for k in firsts]
    return firsts, lasts


def _zero_rows(a, mask, groups):
    pieces, cur = [], 0
    for r in sorted(groups):
        if r > cur:
            pieces.append(a[cur:r])
        pieces.append(jnp.where(mask[r:r + HALO], 0.0, a[r:r + HALO]))
        cur = r + HALO
    if cur < a.shape[0]:
        pieces.append(a[cur:])
    return jnp.concatenate(pieces, axis=0)


def _dwconv3(a_ref, w, first, last, tb, edges):
    prev = a_ref[HALO - 1:HALO - 1 + tb, :]
    mid = a_ref[HALO:HALO + tb, :]
    nxt = a_ref[HALO + 1:HALO + 1 + tb, :]
    prev = _zero_rows(prev, first, edges[0])
    nxt = _zero_rows(nxt, last, edges[1])
    return prev * w[0:1] + mid * w[1:2] + nxt * w[2:3]


def _pos_kernel(o_ref, *, tb, d):
    i = pl.program_id(0)
    q = d // 4
    nr = tb // GRID_W
    j = lax.broadcasted_iota(jnp.int32, (1, q), 1).astype(F32)
    freq = jnp.exp(-math.log(POS_BASE) * j / q)
    rr = (i * nr + lax.broadcasted_iota(jnp.int32, (nr, 1), 0)).astype(F32)
    cc = lax.broadcasted_iota(jnp.int32, (GRID_W, 1), 0).astype(F32)
    ar = rr * freq
    ac = cc * freq
    row_part = jnp.concatenate([jnp.sin(ar), jnp.cos(ar)], axis=1)
    col_part = jnp.concatenate([jnp.sin(ac), jnp.cos(ac)], axis=1)
    for r in range(nr):
        rows = slice(r * GRID_W, (r + 1) * GRID_W)
        o_ref[rows, 0:2 * q] = jnp.broadcast_to(row_part[r:r + 1], (GRID_W, 2 * q))
        o_ref[rows, 2 * q:4 * q] = col_part


def _pos_table(n_tokens, d):
    tb = 512
    assert tb % GRID_W == 0 and n_tokens % tb == 0
    return pl.pallas_call(
        functools.partial(_pos_kernel, tb=tb, d=d),
        out_shape=jax.ShapeDtypeStruct((n_tokens, d), F32),
        grid=(n_tokens // tb,),
        out_specs=pl.BlockSpec((tb, d), lambda i: (i, 0)),
        compiler_params=_cparams(("parallel",)),
        name="pos_table",
    )()


def _ada_kernel(c_ref, w_ref, b_ref, o_ref):
    c = c_ref[...]
    s = (c * jax.nn.sigmoid(c)).astype(BF16)
    o_ref[...] = _dot(s, w_ref[...].astype(BF16)) + b_ref[...]


def _ada_params(cvec, ada_w, ada_b):
    depth, d, n = ada_w.shape
    tn = 1536
    assert n % tn == 0
    return pl.pallas_call(
        _ada_kernel,
        out_shape=jax.ShapeDtypeStruct((depth, 8, n), F32),
        grid=(depth, n // tn),
        in_specs=[
            pl.BlockSpec((8, d), lambda l, j: (0, 0)),
            pl.BlockSpec((None, d, tn), lambda l, j: (l, 0, j)),
            pl.BlockSpec((None, 1, tn), lambda l, j: (l, 0, j)),
        ],
        out_specs=pl.BlockSpec((None, 8, tn), lambda l, j: (l, 0, j)),
        compiler_params=_cparams(("parallel", "parallel")),
        name="ada_params",
    )(cvec, ada_w, ada_b.reshape(depth, 1, n))


def _edge_points(lay, r0, tb):
    is_p = r0 < lay.np_rows
    base = jnp.where(is_p, lax.rem(r0, lay.lp), lax.rem(jnp.maximum(r0 - lay.np_rows, 0), lay.ls))
    L = jnp.where(is_p, lay.lp, lay.ls)
    step = math.gcd(math.gcd(lay.lp, lay.ls), tb)
    firsts = [(k, lax.rem(base + k, L) == 0) for k in range(0, tb, step)]
    lasts = [((k - 1) % tb, lax.rem(base + ((k - 1) % tb) + 1, L) == 0) for k in range(0, tb, step)]
    return firsts, lasts


def _zero_points(a, points, s_rows):
    sub = lax.broadcasted_iota(jnp.int32, (HALO, 1), 0)
    pieces, cur = [], 0
    for k, flag in sorted(points, key=lambda p: (p[0] + HALO) % s_rows):
        s, j = divmod(k + HALO, s_rows)
        r = HALO * j
        assert r >= cur
        if r > cur:
            pieces.append(a[cur:r])
        pieces.append(jnp.where(jnp.logical_and(sub == s, flag), 0.0, a[r:r + HALO]))
        cur = r + HALO
    if cur < a.shape[0]:
        pieces.append(a[cur:])
    return jnp.concatenate(pieces, axis=0)


def _dwconv3_strided(u_ref, w, firsts, lasts):
    n = u_ref.shape[0]
    s_rows = n // HALO
    mid = u_ref[...]
    prev = jnp.concatenate([pltpu.roll(u_ref[n - HALO:n, :], 1, axis=0), u_ref[0:n - HALO, :]], axis=0)
    nxt = jnp.concatenate([u_ref[HALO:n, :], pltpu.roll(u_ref[0:HALO, :], HALO - 1, axis=0)], axis=0)
    prev = _zero_points(prev, firsts, s_rows)
    nxt = _zero_points(nxt, lasts, s_rows)
    return prev * w[0:1] + mid * w[1:2] + nxt * w[2:3]


def _ffn_kernel(*refs, lay, tb, nsub, nf, final, heads):
    refs = list(refs)
    x_ref, xp_ref, xn_ref, mod_ref, g_ref, wup_ref, wdw_ref, wdn_ref = refs[:8]
    del refs[:8]
    if heads:
        mix_refs, ng_ref, wmix_ref = refs[:9], refs[9], refs[10]
        del refs[:11]
    if final:
        fg_ref, yp_ref, ys_ref = refs[:3]
        del refs[:3]
    else:
        o_ref = refs.pop(0)
    slab_ref, u_ref = refs[:2]
    x1_ref = refs[2] if heads else None
    d = lay.d
    fc = FFN_FC
    n = tb + 2 * HALO
    s_rows = n // HALO
    nl = d // 128
    i = pl.program_id(0)
    shift, scale, gate = _mod_parts(mod_ref, d, first=False)

    def halo_rows(m_ref, p_ref, nx_ref, sb, halo):
        lo = sb * tb
        prev = (p_ref[...] if sb == 0 else m_ref[lo - halo:lo, :]).astype(F32)[halo - HALO:halo]
        nxt = (nx_ref[...] if sb == nsub - 1 else m_ref[lo + tb:lo + tb + halo, :]).astype(F32)[0:HALO]
        return jnp.concatenate([prev, m_ref[lo:lo + tb, :].astype(F32), nxt], axis=0)

    def prologue(sb):
        xh = halo_rows(x_ref, xp_ref, xn_ref, sb, HALO)
        if heads:
            dk = d // heads
            o = halo_rows(*mix_refs[0:3], sb, 2 * HALO) + halo_rows(*mix_refs[3:6], sb, 2 * HALO)
            parts = []
            for hd in range(heads):
                oh = o[:, hd * dk:(hd + 1) * dk]
                parts.append(oh * lax.rsqrt(jnp.mean(oh * oh, axis=-1, keepdims=True) + RMS_EPS))
            on = jnp.concatenate(parts, axis=-1) * ng_ref[...]
            y1 = _dot((on * halo_rows(*mix_refs[6:9], sb, 2 * HALO)).astype(BF16), wmix_ref[...])
            xh = xh + mod_ref[:, 2 * d:3 * d] * y1
            x1_ref[sb] = xh
        for l in range(nl):
            slab_ref[sb, 0, l] = xh[:, 128 * l:128 * (l + 1)]
        xs = jnp.concatenate(
            [jnp.concatenate([slab_ref[sb, 0, l, pl.ds(j, HALO, stride=s_rows), :] for l in range(nl)], axis=1)
             for j in range(s_rows)], axis=0)
        return _rms_mod(xs, g_ref[...], shift, scale).astype(BF16)

    def ffn(sb, h):
        firsts, lasts = _edge_points(lay, (i * nsub + sb) * tb, tb)

        def up(c):
            for ab in range(2):
                u_ref[sb, c % 2, ab] = _dot(h, wup_ref[:, ab * nf * fc + c * fc:ab * nf * fc + (c + 1) * fc])

        up(0)
        acc = None
        for c in range(nf):
            if c + 1 < nf:
                up(c + 1)
            ca = _dwconv3_strided(u_ref.at[sb, c % 2, 0], wdw_ref[:, c * fc:(c + 1) * fc], firsts, lasts)
            cb = _dwconv3_strided(u_ref.at[sb, c % 2, 1], wdw_ref[:, (nf + c) * fc:(nf + c + 1) * fc], firsts, lasts)
            act = (ca * jax.nn.sigmoid(ca) * cb).astype(BF16)
            part = _dot(act, wdn_ref[c * fc:(c + 1) * fc, :])
            acc = part if acc is None else acc + part
        return acc

    def epilogue(sb, acc):
        for l in range(nl):
            for j in range(s_rows):
                slab_ref[sb, 1, l, pl.ds(j, HALO, stride=s_rows), :] = (
                    acc[HALO * j:HALO * (j + 1), 128 * l:128 * (l + 1)])
        y = jnp.concatenate([slab_ref[sb, 1, l, HALO:HALO + tb, :] for l in range(nl)], axis=1)
        x_main = x1_ref[sb, HALO:HALO + tb, :] if heads else x_ref[sb * tb:(sb + 1) * tb, :]
        out = x_main + gate * y
        if final:
            out = out * lax.rsqrt(jnp.mean(out * out, axis=-1, keepdims=True) + RMS_EPS) * fg_ref[...]
        else:
            o_ref[sb * tb:(sb + 1) * tb, :] = out
        return out

    hs = [prologue(sb) for sb in range(nsub)]
    accs = [ffn(sb, hs[sb]) for sb in range(nsub)]
    outs = [epilogue(sb, accs[sb]) for sb in range(nsub)]
    if final:
        is_p = i * nsub * tb < lay.np_rows

        @pl.when(is_p)
        def _():
            for sb in range(nsub):
                yp_ref[sb * tb:(sb + 1) * tb, :] = outs[sb]

        @pl.when(jnp.logical_not(is_p))
        def _():
            for sb in range(nsub):
                ys_ref[sb * tb:(sb + 1) * tb, :] = outs[sb]


def _ffn(lay, x, mods, layer, g, wup, wdw, wdn, final_g=None, hgrn=None):
    tb = FFN_TB
    nsub = FFN_NSUB
    tbs = tb * nsub
    d = lay.d
    mix_specs, mix_args, heads = [], [], 0
    if hgrn is not None:
        o_f, o_b, sg, ng, w_mix, j, heads = hgrn
        h16 = 2 * HALO
        r16 = tbs // h16
        last16 = lay.nt // h16 - 1
        for a in (o_f, o_b, sg):
            mix_specs += [pl.BlockSpec((tbs, d), lambda i: (i, 0)),
                          pl.BlockSpec((h16, d), lambda i: (jnp.maximum(i * r16 - 1, 0), 0)),
                          pl.BlockSpec((h16, d), lambda i: (jnp.minimum((i + 1) * r16, last16), 0))]
            mix_args += [a, a, a]
        mix_specs += [_const_spec((1, d)), _layer_spec(w_mix, j)]
        mix_args += [ng, w_mix]
    assert wdn.shape[1] % FFN_FC == 0
    nf = wdn.shape[1] // FFN_FC
    n = tb + 2 * HALO
    lay.check_block(tbs)
    lay.check_block(tb)
    assert n % HALO == 0 and d % 128 == 0 and tb % (2 * HALO) == 0
    final = final_g is not None
    blk = pl.BlockSpec((tbs, d), lambda i: (i, 0))
    if final:
        npb = lay.np_rows // tbs
        out_shape = [jax.ShapeDtypeStruct((lay.np_rows, d), F32), jax.ShapeDtypeStruct((lay.ns_rows, d), F32)]
        out_specs = [pl.BlockSpec((tbs, d), lambda i: (jnp.minimum(i, npb - 1), 0)),
                     pl.BlockSpec((tbs, d), lambda i: (jnp.maximum(i - npb, 0), 0))]
    else:
        out_shape = jax.ShapeDtypeStruct((lay.nt, d), F32)
        out_specs = blk
    return pl.pallas_call(
        functools.partial(_ffn_kernel, lay=lay, tb=tb, nsub=nsub, nf=nf, final=final, heads=heads),
        out_shape=out_shape,
        grid=(lay.nt // tbs,),
        in_specs=_halo_specs(lay, tbs) + [
            _mod_spec(lay, layer, tbs),
            _const_spec((1, d)),
            _layer_spec(wup, layer),
            _layer_spec(wdw, layer),
            _layer_spec(wdn, layer),
        ] + mix_specs + ([_const_spec((1, d))] if final else []),
        out_specs=out_specs,
        scratch_shapes=[pltpu.VMEM((nsub, 2, d // 128, n, 128), F32),
                        pltpu.VMEM((nsub, 2, 2, n, FFN_FC), F32)]
        + ([pltpu.VMEM((nsub, n, d), F32)] if heads else []),
        compiler_params=_cparams(("arbitrary",)),
        name="conv_ffn",
    )(x, x, x, mods, g, wup, wdw, wdn, *mix_args, *([final_g] if final else []))


def _sconv_kernel(*refs, lay, tb, embed):
    if embed:
        (xp_ref, xpp_ref, xpn_ref, xs_ref, xsp_ref, xsn_ref, ps_ref, psp_ref, psn_ref,
         mod_ref, g_ref, win_ref, wdw_ref, wout_ref, o_ref, xh_ref, p_ref) = refs
    else:
        x_ref, xp_ref, xn_ref, mod_ref, g_ref, win_ref, wdw_ref, wout_ref, o_ref, xh_ref, p_ref = refs
    d = lay.d
    n = tb + 2 * HALO
    i = pl.program_id(0)
    pos, L = _pos_in_seq(lay, i * tb, tb)
    first, last = pos == 0, pos == L - 1
    shift, scale, gate = _mod_parts(mod_ref, d, first=True)
    if embed:
        is_p = i * tb < lay.np_rows

        @pl.when(is_p)
        def _():
            xh_ref[0:HALO, :] = xpp_ref[...]
            xh_ref[HALO:HALO + tb, :] = xp_ref[...]
            xh_ref[HALO + tb:n, :] = xpn_ref[...]

        @pl.when(jnp.logical_not(is_p))
        def _():
            xh_ref[0:HALO, :] = xsp_ref[...] + psp_ref[...]
            xh_ref[HALO:HALO + tb, :] = xs_ref[...] + ps_ref[...]
            xh_ref[HALO + tb:n, :] = xsn_ref[...] + psn_ref[...]
    else:
        xh_ref[0:HALO, :] = xp_ref[...]
        xh_ref[HALO:HALO + tb, :] = x_ref[...]
        xh_ref[HALO + tb:n, :] = xn_ref[...]
    h = _rms_mod(xh_ref[...], g_ref[...], shift, scale).astype(BF16)
    bg = _dot(h, win_ref[:, 0:d])[HALO:HALO + tb]
    p_ref[...] = _dot(h, win_ref[:, d:2 * d]) * _dot(h, win_ref[:, 2 * d:3 * d])
    conv = _dwconv3(p_ref, wdw_ref[...], first, last, tb, _edge_groups(lay, tb))
    y = _dot((bg * conv).astype(BF16), wout_ref[...])
    o_ref[...] = xh_ref[HALO:HALO + tb, :] + gate * y


def _stream_halo_specs(rows, d, tb, blk_of):
    r = tb // HALO
    last8 = rows // HALO - 1
    return [
        pl.BlockSpec((tb, d), lambda i: (blk_of(i), 0)),
        pl.BlockSpec((HALO, d), lambda i: (jnp.clip(blk_of(i) * r - 1, 0, last8), 0)),
        pl.BlockSpec((HALO, d), lambda i: (jnp.clip((blk_of(i) + 1) * r, 0, last8), 0)),
    ]


def _sconv_layer(lay, x, mods, layer, g, j, w_in, w_dw, w_out, raw=None):
    tb = TB
    d = lay.d
    embed = raw is not None
    if embed:
        npb = lay.np_rows // tb
        psb = lay.ls // tb
        xspecs = (_stream_halo_specs(lay.np_rows, d, tb, lambda i: jnp.minimum(i, npb - 1))
                  + _stream_halo_specs(lay.ns_rows, d, tb, lambda i: jnp.maximum(i - npb, 0))
                  + _stream_halo_specs(lay.ls, d, tb, lambda i: lax.rem(jnp.maximum(i - npb, 0), psb)))
        xargs = (raw[0],) * 3 + (raw[1],) * 3 + (raw[2],) * 3
    else:
        xspecs = _halo_specs(lay, tb)
        xargs = (x, x, x)
    return pl.pallas_call(
        functools.partial(_sconv_kernel, lay=lay, tb=tb, embed=embed),
        out_shape=jax.ShapeDtypeStruct((lay.nt, d), F32),
        grid=(lay.nt // tb,),
        in_specs=xspecs + [
            _mod_spec(lay, layer, tb),
            _const_spec((1, d)),
            _layer_spec(w_in, j),
            _layer_spec(w_dw, j),
            _layer_spec(w_out, j),
        ],
        out_specs=pl.BlockSpec((tb, d), lambda i: (i, 0)),
        scratch_shapes=[pltpu.VMEM((tb + 2 * HALO, d), F32), pltpu.VMEM((tb + 2 * HALO, d), F32)],
        compiler_params=_cparams(("arbitrary",)),
        name="sconv_mixer",
    )(*xargs, mods, g, w_in, w_dw, w_out)


def _pool_kernel(x_ref, xp_ref, xn_ref, mod_ref, g_ref, w_ref, sc_ref, o_ref, h_ref, *, lay, tb):
    d = lay.d
    ng = len(POOL_WINDOWS)
    pg = d // ng
    i = pl.program_id(0)
    pos, L = _pos_in_seq(lay, i * tb, tb)
    shift, scale, gate = _mod_parts(mod_ref, d, first=True)
    xh = jnp.concatenate([xp_ref[...], x_ref[...], xn_ref[...]], axis=0)
    h_ref[...] = _rms_mod(xh, g_ref[...], shift, scale)
    n = tb + 2 * HALO
    edge_groups = sorted(set(_edge_groups(lay, tb)[0]) | set(_edge_groups(lay, tb)[1]))
    outs = []
    for gi, w in enumerate(POOL_WINDOWS):
        assert w // 2 <= HALO and w & (w - 1) == 0
        cols = slice(gi * pg, (gi + 1) * pg)
        centre = h_ref[HALO:HALO + tb, cols]
        p = h_ref[:, cols]
        m = 1
        while m < w:
            p = p + pltpu.roll(p, m, axis=0)
            m *= 2
        ahead = w // 2 - 1
        if ahead:
            p = pltpu.roll(p, n - ahead, axis=0)
        mean = p[HALO:HALO + tb] * (1.0 / w)
        pieces, cur = [], 0
        for r in edge_groups:
            if r > cur:
                pieces.append(mean[cur:r])
            pr = pos[r:r + HALO]
            s = jnp.zeros((HALO, pg), F32)
            cnt = jnp.zeros((HALO, 1), F32)
            for k in range(-(w // 2), w // 2):
                inside = jnp.logical_and(pr + k >= 0, pr + k <= L - 1)
                s = s + jnp.where(inside, h_ref[HALO + r + k:HALO + r + k + HALO, cols], 0.0)
                cnt = cnt + jnp.where(inside, 1.0, 0.0)
            pieces.append(s / cnt)
            cur = r + HALO
        if cur < tb:
            pieces.append(mean[cur:])
        mean = jnp.concatenate(pieces, axis=0)
        pooled = (mean - centre).astype(BF16)
        outs.append(_dot(pooled, w_ref[gi]))
    y = jnp.concatenate(outs, axis=-1) * sc_ref[...]
    o_ref[...] = x_ref[...] + gate * y


def _pool_layer(lay, x, mods, layer, g, j, w, sc):
    tb = TB
    d = lay.d
    return pl.pallas_call(
        functools.partial(_pool_kernel, lay=lay, tb=tb),
        out_shape=jax.ShapeDtypeStruct((lay.nt, d), F32),
        grid=(lay.nt // tb,),
        in_specs=_halo_specs(lay, tb) + [
            _mod_spec(lay, layer, tb),
            _const_spec((1, d)),
            _layer_spec(w, j),
            _const_spec((1, d)),
        ],
        out_specs=pl.BlockSpec((tb, d), lambda i: (i, 0)),
        scratch_shapes=[pltpu.VMEM((tb + 2 * HALO, d), F32)],
        compiler_params=_cparams(("parallel",)),
        name="pool_mixer",
    )(x, x, x, mods, g, w, sc)


def _sgu_kernel(x_ref, mod_ref, g_ref, win_ref, ng_ref, ws_ref, bs_ref, wout_ref, o_ref, s_ref, *, lay, tb):
    d = lay.d
    groups = ws_ref.shape[0]
    gd = d // groups
    shift, scale, gate = _mod_parts(mod_ref, d, first=True)
    x = x_ref[...]
    h = _rms_mod(x, g_ref[...], shift, scale).astype(BF16)
    v = jax.nn.gelu(_dot(h, win_ref[:, d:2 * d]), approximate=True)
    u = jax.nn.gelu(_dot(h, win_ref[:, 0:d]), approximate=True)
    ms = jnp.mean(v * v, axis=-1, keepdims=True)
    vb = (v * lax.rsqrt(ms + RMS_EPS) * ng_ref[...]).astype(BF16)
    for n in range(tb // SGU_CHUNK):
        for gi in range(groups):
            rows = slice(n * SGU_CHUNK, (n + 1) * SGU_CHUNK)
            cols = slice(gi * gd, (gi + 1) * gd)
            s_ref[rows, cols] = _dot(ws_ref[gi], vb[rows, cols]) + bs_ref[:, gi:gi + 1]
    y = _dot((u * s_ref[...]).astype(BF16), wout_ref[...])
    o_ref[...] = x + gate * y


def _sgu_layer(lay, x, mods, layer, g, j, w_in, norm_g, w_s, b_st, w_out):
    tb = TB
    d = lay.d
    assert tb % SGU_CHUNK == 0 and lay.lp % SGU_CHUNK == 0 and lay.ls % SGU_CHUNK == 0
    return pl.pallas_call(
        functools.partial(_sgu_kernel, lay=lay, tb=tb),
        out_shape=jax.ShapeDtypeStruct((lay.nt, d), F32),
        grid=(lay.nt // tb,),
        in_specs=[
            pl.BlockSpec((tb, d), lambda i: (i, 0)),
            _mod_spec(lay, layer, tb),
            _const_spec((1, d)),
            _layer_spec(w_in, j),
            _const_spec((1, d)),
            _layer_spec(w_s, j),
            _const_spec(b_st.shape),
            _layer_spec(w_out, j),
        ],
        out_specs=pl.BlockSpec((tb, d), lambda i: (i, 0)),
        scratch_shapes=[pltpu.VMEM((tb, d), F32)],
        compiler_params=_cparams(("parallel",)),
        name="sgu_mixer",
    )(x, mods, g, w_in, norm_g, w_s, b_st, w_out)


def _split2(a):
    hi = a.astype(BF16)
    lo = (a - hi.astype(F32)).astype(BF16)
    return hi, lo


def _hgrn_gates(z, lb, one_m_lb):
    ez = jnp.exp(-jnp.abs(z))
    r = 1.0 / (1.0 + ez)
    er = ez * r
    pos = z >= 0.0
    f = lb + one_m_lb * jnp.where(pos, r, er)
    logf = jnp.log(jnp.maximum(f, F32_TINY))
    return logf, one_m_lb * jnp.where(pos, er, r)


def _hgrn_proj_kernel(x_ref, mod_ref, g_ref, win_ref, lb_ref,
                      qf_ref, kf_ref, qb_ref, kb_ref, v_ref, sg_ref, cv_ref, *, lay, tb, layer):
    d = lay.d
    c = HGRN_C
    shift, scale, _ = _mod_parts(mod_ref, d, first=True)
    h = _rms_mod(x_ref[...], g_ref[...], shift, scale).astype(BF16)

    lbp = lb_ref[...]
    e = jnp.exp(lbp - jnp.max(lbp, axis=0, keepdims=True))
    p = e / jnp.sum(e, axis=0, keepdims=True)
    lb = jnp.zeros((1, 2 * d), F32)
    for j in range(1, layer + 1):
        lb = lb + p[j:j + 1]
    one_m_lb = 1.0 - lb

    z = [_dot(h, win_ref[:, (1 + di) * d:(2 + di) * d]) for di in range(2)]
    q = _dot(h, win_ref[:, 0:d])
    gates = [_hgrn_gates(z[di], lb[:, di * d:(di + 1) * d], one_m_lb[:, di * d:(di + 1) * d]) for di in range(2)]

    rt = lax.broadcasted_iota(jnp.int32, (c, 2 * c), 0)
    cs = lax.broadcasted_iota(jnp.int32, (c, 2 * c), 1)
    cs = jnp.where(cs >= c, cs - c, cs)
    tri_f = (cs <= rt).astype(BF16)
    tri_b = (cs >= rt).astype(BF16)
    mid = c // 2

    def cumsum(di, tri):
        logf = gates[di][0]
        return [_dot(tri, jnp.concatenate(_split2(logf[n * c:(n + 1) * c]), axis=0)) for n in range(tb // c)]

    cum = [cumsum(0, tri_f)]
    vv = _dot(h, win_ref[:, 3 * d:4 * d])
    cum.append(cumsum(1, tri_b))
    gg = _dot(h, win_ref[:, 4 * d:5 * d])

    for di, (q_ref, k_ref) in enumerate(((qf_ref, kf_ref), (qb_ref, kb_ref))):
        kk = gates[di][1]
        for n in range(tb // c):
            rows = slice(n * c, (n + 1) * c)
            b = cum[di][n]
            if di == 0:
                btot = b[c - 1:c]
                bref = b[mid - 1:mid]
            else:
                btot = b[0:1]
                bref = b[mid:mid + 1]
            grow = jnp.exp(jnp.clip(b - bref, -EXP_CLAMP, EXP_CLAMP))
            q_ref[rows, :] = (q[rows] * grow).astype(BF16)
            k_ref[rows, :] = (kk[rows] * pl.reciprocal(grow, approx=True)).astype(BF16)
            cv_ref[n, 3 * di:3 * di + 1, :] = jnp.exp(bref)
            cv_ref[n, 3 * di + 1:3 * di + 2, :] = jnp.exp(btot - bref)
            cv_ref[n, 3 * di + 2:3 * di + 3, :] = jnp.exp(btot)
    v_ref[...] = vv.astype(BF16)
    sg_ref[...] = (gg * jax.nn.sigmoid(gg)).astype(BF16)
    for n in range(tb // c):
        cv_ref[n, 6:8, :] = jnp.zeros((2, d), F32)


def _hgrn_proj(lay, x, mods, layer, g, j, w_in, lb2):
    tb = TB
    d = lay.d
    c = HGRN_C
    blk = pl.BlockSpec((tb, d), lambda i: (i, 0))
    act = jax.ShapeDtypeStruct((lay.nt, d), BF16)
    return pl.pallas_call(
        functools.partial(_hgrn_proj_kernel, lay=lay, tb=tb, layer=layer),
        out_shape=[act, act, act, act, act, act, jax.ShapeDtypeStruct((lay.nt // c, 8, d), F32)],
        grid=(lay.nt // tb,),
        in_specs=[blk, _mod_spec(lay, layer, tb), _const_spec((1, d)), _layer_spec(w_in, j),
                  _const_spec(lb2.shape)],
        out_specs=[blk, blk, blk, blk, blk, blk, pl.BlockSpec((tb // c, 8, d), lambda i: (i, 0, 0))],
        compiler_params=_cparams(("parallel",)),
        name="hgrn_proj",
    )(x, mods, g, w_in, lb2)


def _hgrn_scan_kernel(qf_ref, kf_ref, vf_ref, cvf_ref, qb_ref, kb_ref, vb_ref, cvb_ref, s0_ref,
                      of_ref, ob_ref, sfin_ref, st_ref, *, lay, tb, heads):
    d = lay.d
    c = HGRN_C
    dk = d // heads
    i = pl.program_id(0)
    r0 = i * tb
    is_p = r0 < lay.np_rows
    base = jnp.where(is_p, lax.rem(r0, lay.lp), lax.rem(jnp.maximum(r0 - lay.np_rows, 0), lay.ls))
    L = jnp.where(is_p, lay.lp, lay.ls)

    @pl.when(jnp.logical_and(base == 0, is_p))
    def _():
        st_ref[...] = jnp.zeros_like(st_ref)

    @pl.when(jnp.logical_and(base == 0, jnp.logical_not(is_p)))
    def _():
        for di in range(2):
            for hd in range(heads):
                st_ref[di, hd] = s0_ref[di, hd].T

    rt = lax.broadcasted_iota(jnp.int32, (c, c), 0)
    cs = lax.broadcasted_iota(jnp.int32, (c, c), 1)
    keep = (cs <= rt, cs >= rt)
    nchunk = tb // c
    dirs = ((qf_ref, kf_ref, vf_ref, cvf_ref, of_ref), (qb_ref, kb_ref, vb_ref, cvb_ref, ob_ref))
    nt_dims = (((1,), (1,)), ((), ()))
    tn_dims = (((0,), (0,)), ((), ()))
    st = [[st_ref[di, hd] for hd in range(heads)] for di in range(2)]
    for n in range(nchunk):
        part = {}
        for di, (q_ref, k_ref, v_ref, cv_ref, _) in enumerate(dirs):
            nn = n if di == 0 else nchunk - 1 - n
            rows = slice(nn * c, (nn + 1) * c)
            for hd in range(heads):
                cols = slice(hd * dk, (hd + 1) * dk)
                qs, ks, vv = q_ref[rows, cols], k_ref[rows, cols], v_ref[rows, cols]
                ku = (ks.astype(F32) * cv_ref[nn, 3 * di + 1:3 * di + 2, cols]).astype(BF16)
                sc = lax.dot_general(qs, ks, nt_dims, preferred_element_type=F32)
                upd = lax.dot_general(vv, ku, tn_dims, preferred_element_type=F32)
                part[di, hd] = (qs, vv, sc, upd)
        for di, (_, _, _, cv_ref, o_ref) in enumerate(dirs):
            nn = n if di == 0 else nchunk - 1 - n
            rows = slice(nn * c, (nn + 1) * c)
            for hd in range(heads):
                cols = slice(hd * dk, (hd + 1) * dk)
                qs, vv, sc, upd = part[di, hd]
                scb = jnp.where(keep[di], sc, 0.0).astype(BF16)
                qi = (qs.astype(F32) * cv_ref[nn, 3 * di:3 * di + 1, cols]).astype(BF16)
                o_ref[rows, cols] = (_dot(scb, vv) + lax.dot_general(
                    qi, st[di][hd].astype(BF16), nt_dims, preferred_element_type=F32)).astype(BF16)
                st[di][hd] = st[di][hd] * cv_ref[nn, 3 * di + 2:3 * di + 3, cols] + upd
    for di in range(2):
        for hd in range(heads):
            st_ref[di, hd] = st[di][hd]

    @pl.when(jnp.logical_and(base + tb == L, is_p))
    def _():
        for di in range(2):
            for hd in range(heads):
                sfin_ref[di, hd] = st_ref[di, hd].T


def _hgrn_scan(lay, qf, kf, qb, kb, v, cv, s0, heads):
    tb = HGRN_TB
    d = lay.d
    c = HGRN_C
    dk = d // heads
    lay.check_block(tb)
    assert tb <= lay.lp and tb <= lay.ls
    npb = lay.np_rows // tb
    bps_p, bps_s = lay.lp // tb, lay.ls // tb

    def mirror(i):
        ip = (i // bps_p) * bps_p + (bps_p - 1 - lax.rem(i, bps_p))
        j = jnp.maximum(i - npb, 0)
        isx = npb + (j // bps_s) * bps_s + (bps_s - 1 - lax.rem(j, bps_s))
        return jnp.where(i < npb, ip, isx)

    def seq_of(i):
        return jnp.where(i < npb, i // bps_p, lay.batch_p + jnp.maximum(i - npb, 0) // bps_s)

    fwd = pl.BlockSpec((tb, d), lambda i: (i, 0))
    bwd = pl.BlockSpec((tb, d), lambda i: (mirror(i), 0))
    cvf = pl.BlockSpec((tb // c, 8, d), lambda i: (i, 0, 0))
    cvb = pl.BlockSpec((tb // c, 8, d), lambda i: (mirror(i), 0, 0))
    st_blk = (None, 2, heads, dk, dk)
    s0_spec = pl.BlockSpec(st_blk, lambda i: (jnp.clip(seq_of(i) - lay.batch_p, 0, lay.batch_s - 1), 0, 0, 0, 0))
    sfin_spec = pl.BlockSpec(st_blk, lambda i: (jnp.minimum(seq_of(i), lay.batch_p - 1), 0, 0, 0, 0))
    return pl.pallas_call(
        functools.partial(_hgrn_scan_kernel, lay=lay, tb=tb, heads=heads),
        out_shape=[jax.ShapeDtypeStruct((lay.nt, d), BF16), jax.ShapeDtypeStruct((lay.nt, d), BF16),
                   jax.ShapeDtypeStruct((lay.batch_p, 2, heads, dk, dk), F32)],
        grid=(lay.nt // tb,),
        in_specs=[fwd, fwd, fwd, cvf, bwd, bwd, bwd, cvb, s0_spec],
        out_specs=[fwd, bwd, sfin_spec],
        scratch_shapes=[pltpu.VMEM((2, heads, dk, dk), F32)],
        compiler_params=_cparams(("arbitrary",)),
        name="hgrn_scan",
    )(qf, kf, v, cv, qb, kb, v, cv, s0)


def kernel(x_prompt, x_sample, state_rec, c, c_ctx, ada_w, ada_b, norm_g, final_g, conv_w_in, conv_w_dw,
           conv_w_out, pool_w, pool_scale, sgu_w_in, sgu_norm_g, sgu_w_s, sgu_b_s, sgu_w_out, hgrn_w_in,
           hgrn_lb, hgrn_norm_g, hgrn_w_out, ffn_w_up, ffn_w_dw, ffn_w_down):
    bp, lp, d = x_prompt.shape
    bs, ls, _ = x_sample.shape
    depth = ada_w.shape[0]
    heads = state_rec.shape[3]
    lay = _Layout(bp, lp, bs, ls, d)
    lay.check_block(TB)
    assert c.shape[0] == bs and 1 + bs <= 8

    cvec = jnp.concatenate([c_ctx[None], c, jnp.zeros((8 - 1 - bs, d), F32)], axis=0)
    mods = _ada_params(cvec, ada_w, ada_b).reshape(depth * 8, 1, 6 * d)

    assert N_MIXERS >= 1 and depth >= 1
    raw = (x_prompt.reshape(bp * lp, d), x_sample.reshape(bs * ls, d), _pos_table(ls, d))
    x = None

    wb = {k: v.astype(BF16) for k, v in dict(
        conv_in=conv_w_in, conv_out=conv_w_out, pool=pool_w, sgu_in=sgu_w_in, sgu_s=sgu_w_s, sgu_out=sgu_w_out,
        hgrn_in=hgrn_w_in, hgrn_out=hgrn_w_out, ffn_up=ffn_w_up, ffn_down=ffn_w_down).items()}

    new_states = []
    for i in range(depth):
        kind, j = i % N_MIXERS, i // N_MIXERS
        g1 = norm_g[i, 0].reshape(1, d)
        g2 = norm_g[i, 1].reshape(1, d)
        hgrn = None
        if kind == 0:
            x = _sconv_layer(lay, x, mods, i, g1, j, wb["conv_in"], conv_w_dw, wb["conv_out"],
                             raw=raw if i == 0 else None)
        elif kind == 1:
            x = _pool_layer(lay, x, mods, i, g1, j, wb["pool"], pool_scale[j].reshape(1, d))
        elif kind == 2:
            x = _sgu_layer(lay, x, mods, i, g1, j, wb["sgu_in"], sgu_norm_g[j].reshape(1, d),
                           wb["sgu_s"], sgu_b_s[j].T, wb["sgu_out"])
        else:
            qf, kf, qb, kb, v, sg, cv = _hgrn_proj(lay, x, mods, i, g1, j, wb["hgrn_in"],
                                                   hgrn_lb.reshape(depth, 2 * d))
            o_f, o_b, sfin = _hgrn_scan(lay, qf, kf, qb, kb, v, cv, state_rec[:, j], heads)
            new_states.append(sfin)
            hgrn = (o_f, o_b, sg, hgrn_norm_g[j].reshape(1, d), wb["hgrn_out"], j, heads)
        ffn_args = (lay, x, mods, i, g2, wb["ffn_up"], ffn_w_dw, wb["ffn_down"])
        if i + 1 < depth:
            x = _ffn(*ffn_args, hgrn=hgrn)
        else:
            y_prompt, y_sample = _ffn(*ffn_args, final_g=final_g.reshape(1, d), hgrn=hgrn)
    y_prompt = y_prompt.reshape(bp, lp, d)
    y_sample = y_sample.reshape(bs, ls, d)
    new_state_rec = jnp.stack(new_states, axis=1)
    return (y_prompt, y_sample, new_state_rec)
```

```python
import functools
import math

import jax
import jax.numpy as jnp
from jax import lax
from jax.experimental import pallas as pl
from jax.experimental.pallas import tpu as pltpu

GRID_W = 64
N_MIXERS = 4
RMS_EPS = 1e-6
POOL_WINDOWS = (2, 4, 8, 16)
SGU_CHUNK = 128
POS_BASE = 10000.0

HALO = 8
TB = 512
FFN_TB = 256
FFN_NSUB = 2
FFN_FC = 256
HGRN_C = 128
HGRN_TB = 256
EXP_CLAMP = 80.0
F32_TINY = 1e-37
VMEM_LIMIT = 60000 * 1024

BF16 = jnp.bfloat16
F32 = jnp.float32


def _cparams(sem):
    return pltpu.CompilerParams(dimension_semantics=sem, vmem_limit_bytes=VMEM_LIMIT)


def _dot(a, b):
    return jnp.dot(a, b, preferred_element_type=F32)


def _const_spec(shape):
    nd = len(shape)
    return pl.BlockSpec(shape, lambda i: (0,) * nd)


def _layer_spec(stacked, idx):
    rest = stacked.shape[1:]
    return pl.BlockSpec((None,) + rest, lambda i: (idx,) + (0,) * len(rest))


class _Layout:
    def __init__(self, batch_p, seq_p, batch_s, seq_s, d):
        self.lp, self.ls, self.d = seq_p, seq_s, d
        self.np_rows = batch_p * seq_p
        self.ns_rows = batch_s * seq_s
        self.nt = self.np_rows + self.ns_rows
        self.batch_p, self.batch_s = batch_p, batch_s

    def check_block(self, tb):
        assert self.np_rows % tb == 0 and self.ns_rows % tb == 0
        for L in (self.lp, self.ls):
            assert L % tb == 0 or tb % L == 0

    def cond_row(self, r0):
        return jnp.where(r0 < self.np_rows, 0, 1 + (jnp.maximum(r0 - self.np_rows, 0)) // self.ls)


def _pos_in_seq(lay, r0, tb):
    k = lax.broadcasted_iota(jnp.int32, (tb, 1), 0)

    def pos_for(start, L):
        base = lax.rem(jnp.maximum(r0 - start, 0), L)
        p = base + k
        for j in range(1, (tb + L - 1) // L + 1):
            p = p - jnp.where(base + k >= j * L, L, 0)
        return p

    is_p = r0 < lay.np_rows
    pos = jnp.where(is_p, pos_for(0, lay.lp), pos_for(lay.np_rows, lay.ls))
    L = jnp.where(is_p, lay.lp, lay.ls)
    return pos, L


def _rms_mod(x, g, shift, scale):
    ms = jnp.mean(x * x, axis=-1, keepdims=True)
    return x * lax.rsqrt(ms + RMS_EPS) * (g * (1.0 + scale)) + shift


def _mod_parts(mod_ref, d, first):
    o = 0 if first else 3 * d
    return (mod_ref[:, o:o + d], mod_ref[:, o + d:o + 2 * d], mod_ref[:, o + 2 * d:o + 3 * d])


def _halo_specs(lay, tb):
    nb8 = lay.nt // HALO
    r = tb // HALO
    return [
        pl.BlockSpec((tb, lay.d), lambda i: (i, 0)),
        pl.BlockSpec((HALO, lay.d), lambda i: (jnp.maximum(i * r - 1, 0), 0)),
        pl.BlockSpec((HALO, lay.d), lambda i: (jnp.minimum((i + 1) * r, nb8 - 1), 0)),
    ]


def _mod_spec(lay, layer, tb):
    return pl.BlockSpec((None, 1, 6 * lay.d), lambda i: (layer * 8 + lay.cond_row(i * tb), 0, 0))


def _edge_groups(lay, tb):
    step = math.gcd(math.gcd(lay.lp, lay.ls), tb)
    assert step % HALO == 0
    firsts = list(range(0, tb, step))
    lasts = [(k - HALO) % tb for k in firsts]
    return firsts, lasts


def _zero_rows(a, mask, groups):
    pieces, cur = [], 0
    for r in sorted(groups):
        if r > cur:
            pieces.append(a[cur:r])
        pieces.append(jnp.where(mask[r:r + HALO], 0.0, a[r:r + HALO]))
        cur = r + HALO
    if cur < a.shape[0]:
        pieces.append(a[cur:])
    return jnp.concatenate(pieces, axis=0)


def _dwconv3(a_ref, w, first, last, tb, edges):
    prev = a_ref[HALO - 1:HALO - 1 + tb, :]
    mid = a_ref[HALO:HALO + tb, :]
    nxt = a_ref[HALO + 1:HALO + 1 + tb, :]
    prev = _zero_rows(prev, first, edges[0])
    nxt = _zero_rows(nxt, last, edges[1])
    return prev * w[0:1] + mid * w[1:2] + nxt * w[2:3]


def _pos_kernel(o_ref, *, tb, d):
    i = pl.program_id(0)
    q = d // 4
    nr = tb // GRID_W
    j = lax.broadcasted_iota(jnp.int32, (1, q), 1).astype(F32)
    freq = jnp.exp(-math.log(POS_BASE) * j / q)
    rr = (i * nr + lax.broadcasted_iota(jnp.int32, (nr, 1), 0)).astype(F32)
    cc = lax.broadcasted_iota(jnp.int32, (GRID_W, 1), 0).astype(F32)
    ar = rr * freq
    ac = cc * freq
    row_part = jnp.concatenate([jnp.sin(ar), jnp.cos(ar)], axis=1)
    col_part = jnp.concatenate([jnp.sin(ac), jnp.cos(ac)], axis=1)
    for r in range(nr):
        rows = slice(r * GRID_W, (r + 1) * GRID_W)
        o_ref[rows, 0:2 * q] = jnp.broadcast_to(row_part[r:r + 1], (GRID_W, 2 * q))
        o_ref[rows, 2 * q:4 * q] = col_part


def _pos_table(n_tokens, d):
    tb = 512
    assert tb % GRID_W == 0 and n_tokens % tb == 0
    return pl.pallas_call(
        functools.partial(_pos_kernel, tb=tb, d=d),
        out_shape=jax.ShapeDtypeStruct((n_tokens, d), F32),
        grid=(n_tokens // tb,),
        out_specs=pl.BlockSpec((tb, d), lambda i: (i, 0)),
        compiler_params=_cparams(("parallel",)),
        name="pos_table",
    )()


def _ada_kernel(c_ref, w_ref, b_ref, o_ref):
    c = c_ref[...]
    s = (c * jax.nn.sigmoid(c)).astype(BF16)
    o_ref[...] = _dot(s, w_ref[...].astype(BF16)) + b_ref[...]


def _ada_params(cvec, ada_w, ada_b):
    depth, d, n = ada_w.shape
    tn = 1536
    assert n % tn == 0
    return pl.pallas_call(
        _ada_kernel,
        out_shape=jax.ShapeDtypeStruct((depth, 8, n), F32),
        grid=(depth, n // tn),
        in_specs=[
            pl.BlockSpec((8, d), lambda l, j: (0, 0)),
            pl.BlockSpec((None, d, tn), lambda l, j: (l, 0, j)),
            pl.BlockSpec((None, 1, tn), lambda l, j: (l, 0, j)),
        ],
        out_specs=pl.BlockSpec((None, 8, tn), lambda l, j: (l, 0, j)),
        compiler_params=_cparams(("parallel", "parallel")),
        name="ada_params",
    )(cvec, ada_w, ada_b.reshape(depth, 1, n))


def _edge_points(lay, r0, tb):
    is_p = r0 < lay.np_rows
    base = jnp.where(is_p, lax.rem(r0, lay.lp), lax.rem(jnp.maximum(r0 - lay.np_rows, 0), lay.ls))
    L = jnp.where(is_p, lay.lp, lay.ls)
    step = math.gcd(math.gcd(lay.lp, lay.ls), tb)
    firsts = [(k, lax.rem(base + k, L) == 0) for k in range(0, tb, step)]
    lasts = [((k - 1) % tb, lax.rem(base + ((k - 1) % tb) + 1, L) == 0) for k in range(0, tb, step)]
    return firsts, lasts


def _zero_points(a, points, s_rows):
    sub = lax.broadcasted_iota(jnp.int32, (HALO, 1), 0)
    pieces, cur = [], 0
    for k, flag in sorted(points, key=lambda p: (p[0] + HALO) % s_rows):
        s, j = divmod(k + HALO, s_rows)
        r = HALO * j
        assert r >= cur
        if r > cur:
            pieces.append(a[cur:r])
        pieces.append(jnp.where(jnp.logical_and(sub == s, flag), 0.0, a[r:r + HALO]))
        cur = r + HALO
    if cur < a.shape[0]:
        pieces.append(a[cur:])
    return jnp.concatenate(pieces, axis=0)


def _dwconv3_strided(u_ref, w, firsts, lasts):
    n = u_ref.shape[0]
    s_rows = n // HALO
    mid = u_ref[...]
    prev = jnp.concatenate([pltpu.roll(u_ref[n - HALO:n, :], 1, axis=0), u_ref[0:n - HALO, :]], axis=0)
    nxt = jnp.concatenate([u_ref[HALO:n, :], pltpu.roll(u_ref[0:HALO, :], HALO - 1, axis=0)], axis=0)
    prev = _zero_points(prev, firsts, s_rows)
    nxt = _zero_points(nxt, lasts, s_rows)
    return prev * w[0:1] + mid * w[1:2] + nxt * w[2:3]


def _ffn_kernel(*refs, lay, tb, nsub, nf, final, heads):
    refs = list(refs)
    x_ref, xp_ref, xn_ref, mod_ref, g_ref, wup_ref, wdw_ref, wdn_ref = refs[:8]
    del refs[:8]
    if heads:
        mix_refs, ng_ref, wmix_ref = refs[:9], refs[9], refs[10]
        del refs[:11]
    if final:
        fg_ref, yp_ref, ys_ref = refs[:3]
        del refs[:3]
    else:
        o_ref = refs.pop(0)
    slab_ref, u_ref = refs[:2]
    x1_ref = refs[2] if heads else None
    d = lay.d
    fc = FFN_FC
    n = tb + 2 * HALO
    s_rows = n // HALO
    nl = d // 128
    i = pl.program_id(0)
    shift, scale, gate = _mod_parts(mod_ref, d, first=False)

    def halo_rows(m_ref, p_ref, nx_ref, sb, halo):
        lo = sb * tb
        prev = (p_ref[...] if sb == 0 else m_ref[lo - halo:lo, :]).astype(F32)[halo - HALO:halo]
        nxt = (nx_ref[...] if sb == nsub - 1 else m_ref[lo + tb:lo + tb + halo, :]).astype(F32)[0:HALO]
        return jnp.concatenate([prev, m_ref[lo:lo + tb, :].astype(F32), nxt], axis=0)

    def prologue(sb):
        xh = halo_rows(x_ref, xp_ref, xn_ref, sb, HALO)
        if heads:
            dk = d // heads
            o = halo_rows(*mix_refs[0:3], sb, 2 * HALO) + halo_rows(*mix_refs[3:6], sb, 2 * HALO)
            parts = []
            for hd in range(heads):
                oh = o[:, hd * dk:(hd + 1) * dk]
                parts.append(oh * lax.rsqrt(jnp.mean(oh * oh, axis=-1, keepdims=True) + RMS_EPS))
            on = jnp.concatenate(parts, axis=-1) * ng_ref[...]
            y1 = _dot((on * halo_rows(*mix_refs[6:9], sb, 2 * HALO)).astype(BF16), wmix_ref[...])
            xh = xh + mod_ref[:, 2 * d:3 * d] * y1
            x1_ref[sb] = xh
        for l in range(nl):
            slab_ref[sb, 0, l] = xh[:, 128 * l:128 * (l + 1)]
        xs = jnp.concatenate(
            [jnp.concatenate([slab_ref[sb, 0, l, pl.ds(j, HALO, stride=s_rows), :] for l in range(nl)], axis=1)
             for j in range(s_rows)], axis=0)
        return _rms_mod(xs, g_ref[...], shift, scale).astype(BF16)

    def ffn(sb, h):
        firsts, lasts = _edge_points(lay, (i * nsub + sb) * tb, tb)

        def up(c):
            for ab in range(2):
                u_ref[sb, c % 2, ab] = _dot(h, wup_ref[:, ab * nf * fc + c * fc:ab * nf * fc + (c + 1) * fc])

        up(0)
        acc = None
        for c in range(nf):
            if c + 1 < nf:
                up(c + 1)
            ca = _dwconv3_strided(u_ref.at[sb, c % 2, 0], wdw_ref[:, c * fc:(c + 1) * fc], firsts, lasts)
            cb = _dwconv3_strided(u_ref.at[sb, c % 2, 1], wdw_ref[:, (nf + c) * fc:(nf + c + 1) * fc], firsts, lasts)
            act = (ca * jax.nn.sigmoid(ca) * cb).astype(BF16)
            part = _dot(act, wdn_ref[c * fc:(c + 1) * fc, :])
            acc = part if acc is None else acc + part
        return acc

    def epilogue(sb, acc):
        for l in range(nl):
            for j in range(s_rows):
                slab_ref[sb, 1, l, pl.ds(j, HALO, stride=s_rows), :] = (
                    acc[HALO * j:HALO * (j + 1), 128 * l:128 * (l + 1)])
        y = jnp.concatenate([slab_ref[sb, 1, l, HALO:HALO + tb, :] for l in range(nl)], axis=1)
        x_main = x1_ref[sb, HALO:HALO + tb, :] if heads else x_ref[sb * tb:(sb + 1) * tb, :]
        out = x_main + gate * y
        if final:
            out = out * lax.rsqrt(jnp.mean(out * out, axis=-1, keepdims=True) + RMS_EPS) * fg_ref[...]
        else:
            o_ref[sb * tb:(sb + 1) * tb, :] = out
        return out

    hs = [prologue(sb) for sb in range(nsub)]
    accs = [ffn(sb, hs[sb]) for sb in range(nsub)]
    outs = [epilogue(sb, accs[sb]) for sb in range(nsub)]
    if final:
        is_p = i * nsub * tb < lay.np_rows

        @pl.when(is_p)
        def _():
            for sb in range(nsub):
                yp_ref[sb * tb:(sb + 1) * tb, :] = outs[sb]

        @pl.when(jnp.logical_not(is_p))
        def _():
            for sb in range(nsub):
                ys_ref[sb * tb:(sb + 1) * tb, :] = outs[sb]


def _ffn(lay, x, mods, layer, g, wup, wdw, wdn, final_g=None, hgrn=None):
    tb = FFN_TB
    nsub = FFN_NSUB
    tbs = tb * nsub
    d = lay.d
    mix_specs, mix_args, heads = [], [], 0
    if hgrn is not None:
        o_f, o_b, sg, ng, w_mix, j, heads = hgrn
        h16 = 2 * HALO
        r16 = tbs // h16
        last16 = lay.nt // h16 - 1
        for a in (o_f, o_b, sg):
            mix_specs += [pl.BlockSpec((tbs, d), lambda i: (i, 0)),
                          pl.BlockSpec((h16, d), lambda i: (jnp.maximum(i * r16 - 1, 0), 0)),
                          pl.BlockSpec((h16, d), lambda i: (jnp.minimum((i + 1) * r16, last16), 0))]
            mix_args += [a, a, a]
        mix_specs += [_const_spec((1, d)), _layer_spec(w_mix, j)]
        mix_args += [ng, w_mix]
    assert wdn.shape[1] % FFN_FC == 0
    nf = wdn.shape[1] // FFN_FC
    n = tb + 2 * HALO
    lay.check_block(tbs)
    lay.check_block(tb)
    assert n % HALO == 0 and d % 128 == 0 and tb % (2 * HALO) == 0
    final = final_g is not None
    blk = pl.BlockSpec((tbs, d), lambda i: (i, 0))
    if final:
        npb = lay.np_rows // tbs
        out_shape = [jax.ShapeDtypeStruct((lay.np_rows, d), F32), jax.ShapeDtypeStruct((lay.ns_rows, d), F32)]
        out_specs = [pl.BlockSpec((tbs, d), lambda i: (jnp.minimum(i, npb - 1), 0)),
                     pl.BlockSpec((tbs, d), lambda i: (jnp.maximum(i - npb, 0), 0))]
    else:
        out_shape = jax.ShapeDtypeStruct((lay.nt, d), F32)
        out_specs = blk
    return pl.pallas_call(
        functools.partial(_ffn_kernel, lay=lay, tb=tb, nsub=nsub, nf=nf, final=final, heads=heads),
        out_shape=out_shape,
        grid=(lay.nt // tbs,),
        in_specs=_halo_specs(lay, tbs) + [
            _mod_spec(lay, layer, tbs),
            _const_spec((1, d)),
            _layer_spec(wup, layer),
            _layer_spec(wdw, layer),
            _layer_spec(wdn, layer),
        ] + mix_specs + ([_const_spec((1, d))] if final else []),
        out_specs=out_specs,
        scratch_shapes=[pltpu.VMEM((nsub, 2, d // 128, n, 128), F32),
                        pltpu.VMEM((nsub, 2, 2, n, FFN_FC), F32)]
        + ([pltpu.VMEM((nsub, n, d), F32)] if heads else []),
        compiler_params=_cparams(("arbitrary",)),
        name="conv_ffn",
    )(x, x, x, mods, g, wup, wdw, wdn, *mix_args, *([final_g] if final else []))


def _sconv_kernel(*refs, lay, tb, embed):
    if embed:
        (xp_ref, xpp_ref, xpn_ref, xs_ref, xsp_ref, xsn_ref, ps_ref, psp_ref, psn_ref,
         mod_ref, g_ref, win_ref, wdw_ref, wout_ref, o_ref, xh_ref, p_ref) = refs
    else:
        x_ref, xp_ref, xn_ref, mod_ref, g_ref, win_ref, wdw_ref, wout_ref, o_ref, xh_ref, p_ref = refs
    d = lay.d
    n = tb + 2 * HALO
    i = pl.program_id(0)
    pos, L = _pos_in_seq(lay, i * tb, tb)
    first, last = pos == 0, pos == L - 1
    shift, scale, gate = _mod_parts(mod_ref, d, first=True)
    if embed:
        is_p = i * tb < lay.np_rows

        @pl.when(is_p)
        def _():
            xh_ref[0:HALO, :] = xpp_ref[...]
            xh_ref[HALO:HALO + tb, :] = xp_ref[...]
            xh_ref[HALO + tb:n, :] = xpn_ref[...]

        @pl.when(jnp.logical_not(is_p))
        def _():
            xh_ref[0:HALO, :] = xsp_ref[...] + psp_ref[...]
            xh_ref[HALO:HALO + tb, :] = xs_ref[...] + ps_ref[...]
            xh_ref[HALO + tb:n, :] = xsn_ref[...] + psn_ref[...]
    else:
        xh_ref[0:HALO, :] = xp_ref[...]
        xh_ref[HALO:HALO + tb, :] = x_ref[...]
        xh_ref[HALO + tb:n, :] = xn_ref[...]
    h = _rms_mod(xh_ref[...], g_ref[...], shift, scale).astype(BF16)
    bg = _dot(h, win_ref[:, 0:d])[HALO:HALO + tb]
    p_ref[...] = _dot(h, win_ref[:, d:2 * d]) * _dot(h, win_ref[:, 2 * d:3 * d])
    conv = _dwconv3(p_ref, wdw_ref[...], first, last, tb, _edge_groups(lay, tb))
    y = _dot((bg * conv).astype(BF16), wout_ref[...])
    o_ref[...] = xh_ref[HALO:HALO + tb, :] + gate * y


def _stream_halo_specs(rows, d, tb, blk_of):
    r = tb // HALO
    last8 = rows // HALO - 1
    return [
        pl.BlockSpec((tb, d), lambda i: (blk_of(i), 0)),
        pl.BlockSpec((HALO, d), lambda i: (jnp.clip(blk_of(i) * r - 1, 0, last8), 0)),
        pl.BlockSpec((HALO, d), lambda i: (jnp.clip((blk_of(i) + 1) * r, 0, last8), 0)),
    ]


def _sconv_layer(lay, x, mods, layer, g, j, w_in, w_dw, w_out, raw=None):
    tb = TB
    d = lay.d
    embed = raw is not None
    if embed:
        npb = lay.np_rows // tb
        psb = lay.ls // tb
        xspecs = (_stream_halo_specs(lay.np_rows, d, tb, lambda i: jnp.minimum(i, npb - 1))
                  + _stream_halo_specs(lay.ns_rows, d, tb, lambda i: jnp.maximum(i - npb, 0))
                  + _stream_halo_specs(lay.ls, d, tb, lambda i: lax.rem(jnp.maximum(i - npb, 0), psb)))
        xargs = (raw[0],) * 3 + (raw[1],) * 3 + (raw[2],) * 3
    else:
        xspecs = _halo_specs(lay, tb)
        xargs = (x, x, x)
    return pl.pallas_call(
        functools.partial(_sconv_kernel, lay=lay, tb=tb, embed=embed),
        out_shape=jax.ShapeDtypeStruct((lay.nt, d), F32),
        grid=(lay.nt // tb,),
        in_specs=xspecs + [
            _mod_spec(lay, layer, tb),
            _const_spec((1, d)),
            _layer_spec(w_in, j),
            _layer_spec(w_dw, j),
            _layer_spec(w_out, j),
        ],
        out_specs=pl.BlockSpec((tb, d), lambda i: (i, 0)),
        scratch_shapes=[pltpu.VMEM((tb + 2 * HALO, d), F32), pltpu.VMEM((tb + 2 * HALO, d), F32)],
        compiler_params=_cparams(("arbitrary",)),
        name="sconv_mixer",
    )(*xargs, mods, g, w_in, w_dw, w_out)


def _pool_kernel(x_ref, xp_ref, xn_ref, mod_ref, g_ref, w_ref, sc_ref, o_ref, h_ref, *, lay, tb):
    d = lay.d
    ng = len(POOL_WINDOWS)
    pg = d // ng
    i = pl.program_id(0)
    pos, L = _pos_in_seq(lay, i * tb, tb)
    shift, scale, gate = _mod_parts(mod_ref, d, first=True)
    xh = jnp.concatenate([xp_ref[...], x_ref[...], xn_ref[...]], axis=0)
    h_ref[...] = _rms_mod(xh, g_ref[...], shift, scale)
    n = tb + 2 * HALO
    edge_groups = sorted(set(_edge_groups(lay, tb)[0]) | set(_edge_groups(lay, tb)[1]))
    outs = []
    for gi, w in enumerate(POOL_WINDOWS):
        assert w // 2 <= HALO and w & (w - 1) == 0
        cols = slice(gi * pg, (gi + 1) * pg)
        centre = h_ref[HALO:HALO + tb, cols]
        p = h_ref[:, cols]
        m = 1
        while m < w:
            p = p + pltpu.roll(p, m, axis=0)
            m *= 2
        ahead = w // 2 - 1
        if ahead:
            p = pltpu.roll(p, n - ahead, axis=0)
        mean = p[HALO:HALO + tb] * (1.0 / w)
        pieces, cur = [], 0
        for r in edge_groups:
            if r > cur:
                pieces.append(mean[cur:r])
            pr = pos[r:r + HALO]
            s = jnp.zeros((HALO, pg), F32)
            cnt = jnp.zeros((HALO, 1), F32)
            for k in range(-(w // 2), w // 2):
                inside = jnp.logical_and(pr + k >= 0, pr + k <= L - 1)
                s = s + jnp.where(inside, h_ref[HALO + r + k:HALO + r + k + HALO, cols], 0.0)
                cnt = cnt + jnp.where(inside, 1.0, 0.0)
            pieces.append(s / cnt)
            cur = r + HALO
        if cur < tb:
            pieces.append(mean[cur:])
        mean = jnp.concatenate(pieces, axis=0)
        pooled = (mean - centre).astype(BF16)
        outs.append(_dot(pooled, w_ref[gi]))
    y = jnp.concatenate(outs, axis=-1) * sc_ref[...]
    o_ref[...] = x_ref[...] + gate * y


def _pool_layer(lay, x, mods, layer, g, j, w, sc):
    tb = TB
    d = lay.d
    return pl.pallas_call(
        functools.partial(_pool_kernel, lay=lay, tb=tb),
        out_shape=jax.ShapeDtypeStruct((lay.nt, d), F32),
        grid=(lay.nt // tb,),
        in_specs=_halo_specs(lay, tb) + [
            _mod_spec(lay, layer, tb),
            _const_spec((1, d)),
            _layer_spec(w, j),
            _const_spec((1, d)),
        ],
        out_specs=pl.BlockSpec((tb, d), lambda i: (i, 0)),
        scratch_shapes=[pltpu.VMEM((tb + 2 * HALO, d), F32)],
        compiler_params=_cparams(("parallel",)),
        name="pool_mixer",
    )(x, x, x, mods, g, w, sc)


def _sgu_kernel(x_ref, mod_ref, g_ref, win_ref, ng_ref, ws_ref, bs_ref, wout_ref, o_ref, s_ref, *, lay, tb):
    d = lay.d
    groups = ws_ref.shape[0]
    gd = d // groups
    shift, scale, gate = _mod_parts(mod_ref, d, first=True)
    x = x_ref[...]
    h = _rms_mod(x, g_ref[...], shift, scale).astype(BF16)
    v = jax.nn.gelu(_dot(h, win_ref[:, d:2 * d]), approximate=True)
    u = jax.nn.gelu(_dot(h, win_ref[:, 0:d]), approximate=True)
    ms = jnp.mean(v * v, axis=-1, keepdims=True)
    vb = (v * lax.rsqrt(ms + RMS_EPS) * ng_ref[...]).astype(BF16)
    nchunks = tb // SGU_CHUNK
    for gi in range(groups):
        cols = slice(gi * gd, (gi + 1) * gd)
        side = jnp.concatenate([vb[n * SGU_CHUNK:(n + 1) * SGU_CHUNK, cols] for n in range(nchunks)], axis=1)
        s_all = _dot(ws_ref[gi], side) + bs_ref[:, gi:gi + 1]
        for n in range(nchunks):
            s_ref[n * SGU_CHUNK:(n + 1) * SGU_CHUNK, cols] = s_all[:, n * gd:(n + 1) * gd]
    y = _dot((u * s_ref[...]).astype(BF16), wout_ref[...])
    o_ref[...] = x + gate * y


def _sgu_layer(lay, x, mods, layer, g, j, w_in, norm_g, w_s, b_st, w_out):
    tb = TB
    d = lay.d
    assert tb % SGU_CHUNK == 0 and lay.lp % SGU_CHUNK == 0 and lay.ls % SGU_CHUNK == 0
    return pl.pallas_call(
        functools.partial(_sgu_kernel, lay=lay, tb=tb),
        out_shape=jax.ShapeDtypeStruct((lay.nt, d), F32),
        grid=(lay.nt // tb,),
        in_specs=[
            pl.BlockSpec((tb, d), lambda i: (i, 0)),
            _mod_spec(lay, layer, tb),
            _const_spec((1, d)),
            _layer_spec(w_in, j),
            _const_spec((1, d)),
            _layer_spec(w_s, j),
            _const_spec(b_st.shape),
            _layer_spec(w_out, j),
        ],
        out_specs=pl.BlockSpec((tb, d), lambda i: (i, 0)),
        scratch_shapes=[pltpu.VMEM((tb, d), F32)],
        compiler_params=_cparams(("parallel",)),
        name="sgu_mixer",
    )(x, mods, g, w_in, norm_g, w_s, b_st, w_out)


def _split2(a):
    hi = a.astype(BF16)
    lo = (a - hi.astype(F32)).astype(BF16)
    return hi, lo


def _hgrn_gates(z, lb, one_m_lb):
    ez = jnp.exp(-jnp.abs(z))
    r = 1.0 / (1.0 + ez)
    er = ez * r
    pos = z >= 0.0
    f = lb + one_m_lb * jnp.where(pos, r, er)
    logf = jnp.log(jnp.maximum(f, F32_TINY))
    return logf, one_m_lb * jnp.where(pos, er, r)


def _hgrn_proj_kernel(x_ref, mod_ref, g_ref, win_ref, lb_ref,
                      qf_ref, kf_ref, qb_ref, kb_ref, v_ref, sg_ref, cv_ref, *, lay, tb, layer):
    d = lay.d
    c = HGRN_C
    shift, scale, _ = _mod_parts(mod_ref, d, first=True)
    h = _rms_mod(x_ref[...], g_ref[...], shift, scale).astype(BF16)

    lbp = lb_ref[...]
    e = jnp.exp(lbp - jnp.max(lbp, axis=0, keepdims=True))
    p = e / jnp.sum(e, axis=0, keepdims=True)
    lb = jnp.zeros((1, 2 * d), F32)
    for j in range(1, layer + 1):
        lb = lb + p[j:j + 1]
    one_m_lb = 1.0 - lb

    z = [_dot(h, win_ref[:, (1 + di) * d:(2 + di) * d]) for di in range(2)]
    q = _dot(h, win_ref[:, 0:d])
    gates = [[_hgrn_gates(z[di][n * c:(n + 1) * c], lb[:, di * d:(di + 1) * d], one_m_lb[:, di * d:(di + 1) * d])
              for n in range(tb // c)] for di in range(2)]

    rt = lax.broadcasted_iota(jnp.int32, (c, 2 * c), 0)
    cs = lax.broadcasted_iota(jnp.int32, (c, 2 * c), 1)
    cs = jnp.where(cs >= c, cs - c, cs)
    tri_f = (cs <= rt).astype(BF16)
    tri_b = (cs >= rt).astype(BF16)
    mid = c // 2

    def cumsum(di, tri):
        return [_dot(tri, jnp.concatenate(_split2(gates[di][n][0]), axis=0)) for n in range(tb // c)]

    cum = [cumsum(0, tri_f)]
    vv = _dot(h, win_ref[:, 3 * d:4 * d])
    cum.append(cumsum(1, tri_b))
    gg = _dot(h, win_ref[:, 4 * d:5 * d])

    for di, (q_ref, k_ref) in enumerate(((qf_ref, kf_ref), (qb_ref, kb_ref))):
        for n in range(tb // c):
            rows = slice(n * c, (n + 1) * c)
            b = cum[di][n]
            if di == 0:
                btot = b[c - 1:c]
                bref = b[mid - 1:mid]
            else:
                btot = b[0:1]
                bref = b[mid:mid + 1]
            grow = jnp.exp(jnp.clip(b - bref, -EXP_CLAMP, EXP_CLAMP))
            q_ref[rows, :] = (q[rows] * grow).astype(BF16)
            k_ref[rows, :] = (gates[di][n][1] * pl.reciprocal(grow, approx=True)).astype(BF16)
            cv_ref[n, 3 * di:3 * di + 1, :] = jnp.exp(bref)
            cv_ref[n, 3 * di + 1:3 * di + 2, :] = jnp.exp(btot - bref)
            cv_ref[n, 3 * di + 2:3 * di + 3, :] = jnp.exp(btot)
    v_ref[...] = vv.astype(BF16)
    sg_ref[...] = (gg * jax.nn.sigmoid(gg)).astype(BF16)
    for n in range(tb // c):
        cv_ref[n, 6:8, :] = jnp.zeros((2, d), F32)


def _hgrn_proj(lay, x, mods, layer, g, j, w_in, lb2):
    tb = TB
    d = lay.d
    c = HGRN_C
    blk = pl.BlockSpec((tb, d), lambda i: (i, 0))
    act = jax.ShapeDtypeStruct((lay.nt, d), BF16)
    return pl.pallas_call(
        functools.partial(_hgrn_proj_kernel, lay=lay, tb=tb, layer=layer),
        out_shape=[act, act, act, act, act, act, jax.ShapeDtypeStruct((lay.nt // c, 8, d), F32)],
        grid=(lay.nt // tb,),
        in_specs=[blk, _mod_spec(lay, layer, tb), _const_spec((1, d)), _layer_spec(w_in, j),
                  _const_spec(lb2.shape)],
        out_specs=[blk, blk, blk, blk, blk, blk, pl.BlockSpec((tb // c, 8, d), lambda i: (i, 0, 0))],
        compiler_params=_cparams(("parallel",)),
        name="hgrn_proj",
    )(x, mods, g, w_in, lb2)


def _hgrn_scan_kernel(qf_ref, kf_ref, vf_ref, cvf_ref, qb_ref, kb_ref, vb_ref, cvb_ref, s0_ref,
                      of_ref, ob_ref, sfin_ref, st_ref, *, lay, tb, heads):
    d = lay.d
    c = HGRN_C
    dk = d // heads
    i = pl.program_id(0)
    r0 = i * tb
    is_p = r0 < lay.np_rows
    base = jnp.where(is_p, lax.rem(r0, lay.lp), lax.rem(jnp.maximum(r0 - lay.np_rows, 0), lay.ls))
    L = jnp.where(is_p, lay.lp, lay.ls)

    @pl.when(jnp.logical_and(base == 0, is_p))
    def _():
        st_ref[...] = jnp.zeros_like(st_ref)

    @pl.when(jnp.logical_and(base == 0, jnp.logical_not(is_p)))
    def _():
        for di in range(2):
            for hd in range(heads):
                st_ref[di, hd] = s0_ref[di, hd].T

    rt = lax.broadcasted_iota(jnp.int32, (c, c), 0)
    cs = lax.broadcasted_iota(jnp.int32, (c, c), 1)
    keep = (cs <= rt, cs >= rt)
    nchunk = tb // c
    dirs = ((qf_ref, kf_ref, vf_ref, cvf_ref, of_ref), (qb_ref, kb_ref, vb_ref, cvb_ref, ob_ref))
    nt_dims = (((1,), (1,)), ((), ()))
    tn_dims = (((0,), (0,)), ((), ()))
    st = [[st_ref[di, hd] for hd in range(heads)] for di in range(2)]
    for n in range(nchunk):
        part = {}
        for di, (q_ref, k_ref, v_ref, cv_ref, _) in enumerate(dirs):
            nn = n if di == 0 else nchunk - 1 - n
            rows = slice(nn * c, (nn + 1) * c)
            for hd in range(heads):
                cols = slice(hd * dk, (hd + 1) * dk)
                qs, ks, vv = q_ref[rows, cols], k_ref[rows, cols], v_ref[rows, cols]
                ku = (ks.astype(F32) * cv_ref[nn, 3 * di + 1:3 * di + 2, cols]).astype(BF16)
                sc = lax.dot_general(qs, ks, nt_dims, preferred_element_type=F32)
                upd = lax.dot_general(vv, ku, tn_dims, preferred_element_type=F32)
                part[di, hd] = (qs, vv, sc, upd)
        for di, (_, _, _, cv_ref, o_ref) in enumerate(dirs):
            nn = n if di == 0 else nchunk - 1 - n
            rows = slice(nn * c, (nn + 1) * c)
            for hd in range(heads):
                cols = slice(hd * dk, (hd + 1) * dk)
                qs, vv, sc, upd = part[di, hd]
                scb = jnp.where(keep[di], sc, 0.0).astype(BF16)
                qi = (qs.astype(F32) * cv_ref[nn, 3 * di:3 * di + 1, cols]).astype(BF16)
                o_ref[rows, cols] = (_dot(scb, vv) + lax.dot_general(
                    qi, st[di][hd].astype(BF16), nt_dims, preferred_element_type=F32)).astype(BF16)
                st[di][hd] = st[di][hd] * cv_ref[nn, 3 * di + 2:3 * di + 3, cols] + upd
    for di in range(2):
        for hd in range(heads):
            st_ref[di, hd] = st[di][hd]

    @pl.when(jnp.logical_and(base + tb == L, is_p))
    def _():
        for di in range(2):
            for hd in range(heads):
                sfin_ref[di, hd] = st_ref[di, hd].T


def _hgrn_scan(lay, qf, kf, qb, kb, v, cv, s0, heads):
    tb = HGRN_TB
    d = lay.d
    c = HGRN_C
    dk = d // heads
    lay.check_block(tb)
    assert tb <= lay.lp and tb <= lay.ls
    npb = lay.np_rows // tb
    bps_p, bps_s = lay.lp // tb, lay.ls // tb

    def mirror(i):
        ip = (i // bps_p) * bps_p + (bps_p - 1 - lax.rem(i, bps_p))
        j = jnp.maximum(i - npb, 0)
        isx = npb + (j // bps_s) * bps_s + (bps_s - 1 - lax.rem(j, bps_s))
        return jnp.where(i < npb, ip, isx)

    def seq_of(i):
        return jnp.where(i < npb, i // bps_p, lay.batch_p + jnp.maximum(i - npb, 0) // bps_s)

    fwd = pl.BlockSpec((tb, d), lambda i: (i, 0))
    bwd = pl.BlockSpec((tb, d), lambda i: (mirror(i), 0))
    cvf = pl.BlockSpec((tb // c, 8, d), lambda i: (i, 0, 0))
    cvb = pl.BlockSpec((tb // c, 8, d), lambda i: (mirror(i), 0, 0))
    st_blk = (None, 2, heads, dk, dk)
    s0_spec = pl.BlockSpec(st_blk, lambda i: (jnp.clip(seq_of(i) - lay.batch_p, 0, lay.batch_s - 1), 0, 0, 0, 0))
    sfin_spec = pl.BlockSpec(st_blk, lambda i: (jnp.minimum(seq_of(i), lay.batch_p - 1), 0, 0, 0, 0))
    return pl.pallas_call(
        functools.partial(_hgrn_scan_kernel, lay=lay, tb=tb, heads=heads),
        out_shape=[jax.ShapeDtypeStruct((lay.nt, d), BF16), jax.ShapeDtypeStruct((lay.nt, d), BF16),
                   jax.ShapeDtypeStruct((lay.batch_p, 2, heads, dk, dk), F32)],
        grid=(lay.nt // tb,),
        in_specs=[fwd, fwd, fwd, cvf, bwd, bwd, bwd, cvb, s0_spec],
        out_specs=[fwd, bwd, sfin_spec],
        scratch_shapes=[pltpu.VMEM((2, heads, dk, dk), F32)],
        compiler_params=_cparams(("arbitrary",)),
        name="hgrn_scan",
    )(qf, kf, v, cv, qb, kb, v, cv, s0)


def kernel(x_prompt, x_sample, state_rec, c, c_ctx, ada_w, ada_b, norm_g, final_g, conv_w_in, conv_w_dw,
           conv_w_out, pool_w, pool_scale, sgu_w_in, sgu_norm_g, sgu_w_s, sgu_b_s, sgu_w_out, hgrn_w_in,
           hgrn_lb, hgrn_norm_g, hgrn_w_out, ffn_w_up, ffn_w_dw, ffn_w_down):
    bp, lp, d = x_prompt.shape
    bs, ls, _ = x_sample.shape
    depth = ada_w.shape[0]
    heads = state_rec.shape[3]
    lay = _Layout(bp, lp, bs, ls, d)
    lay.check_block(TB)
    assert c.shape[0] == bs and 1 + bs <= 8

    cvec = jnp.concatenate([c_ctx[None], c, jnp.zeros((8 - 1 - bs, d), F32)], axis=0)
    mods = _ada_params(cvec, ada_w, ada_b).reshape(depth * 8, 1, 6 * d)

    assert N_MIXERS >= 1 and depth >= 1
    raw = (x_prompt.reshape(bp * lp, d), x_sample.reshape(bs * ls, d), _pos_table(ls, d))
    x = None

    wb = {k: v.astype(BF16) for k, v in dict(
        conv_in=conv_w_in, conv_out=conv_w_out, pool=pool_w, sgu_in=sgu_w_in, sgu_s=sgu_w_s, sgu_out=sgu_w_out,
        hgrn_in=hgrn_w_in, hgrn_out=hgrn_w_out, ffn_up=ffn_w_up, ffn_down=ffn_w_down).items()}

    new_states = []
    for i in range(depth):
        kind, j = i % N_MIXERS, i // N_MIXERS
        g1 = norm_g[i, 0].reshape(1, d)
        g2 = norm_g[i, 1].reshape(1, d)
        hgrn = None
        if kind == 0:
            x = _sconv_layer(lay, x, mods, i, g1, j, wb["conv_in"], conv_w_dw, wb["conv_out"],
                             raw=raw if i == 0 else None)
        elif kind == 1:
            x = _pool_layer(lay, x, mods, i, g1, j, wb["pool"], pool_scale[j].reshape(1, d))
        elif kind == 2:
            x = _sgu_layer(lay, x, mods, i, g1, j, wb["sgu_in"], sgu_norm_g[j].reshape(1, d),
                           wb["sgu_s"], sgu_b_s[j].T, wb["sgu_out"])
        else:
            qf, kf, qb, kb, v, sg, cv = _hgrn_proj(lay, x, mods, i, g1, j, wb["hgrn_in"],
                                                   hgrn_lb.reshape(depth, 2 * d))
            o_f, o_b, sfin = _hgrn_scan(lay, qf, kf, qb, kb, v, cv, state_rec[:, j], heads)
            new_states.append(sfin)
            hgrn = (o_f, o_b, sg, hgrn_norm_g[j].reshape(1, d), wb["hgrn_out"], j, heads)
        ffn_args = (lay, x, mods, i, g2, wb["ffn_up"], ffn_w_dw, wb["ffn_down"])
        if i + 1 < depth:
            x = _ffn(*ffn_args, hgrn=hgrn)
        else:
            y_prompt, y_sample = _ffn(*ffn_args, final_g=final_g.reshape(1, d), hgrn=hgrn)
    y_prompt = y_prompt.reshape(bp, lp, d)
    y_sample = y_sample.reshape(bs, ls, d)
    new_state_rec = jnp.stack(new_states, axis=1)
    return (y_prompt, y_sample, new_state_rec)
```
